```python
import jax, jax.numpy as jnp
from jax import lax
import numpy as np

D_MODEL = 2048
BATCH = 8
SEQ = 2048
DEPTH = 4

CHUNK = 64
MIX_WIDTH = D_MODEL
GROUP_WIDTH = MIX_WIDTH // 4
CONV_K = 3
SB_HEADS = 4
SB_HEAD_DIM = GROUP_WIDTH // SB_HEADS
SB_BLOCK = 128
GLA_HEADS = 4
GLA_DV = GROUP_WIDTH // GLA_HEADS
GLA_DK = GLA_DV // 2
GLA_GATE_RANK = 16
GLA_TAU = 16.0
POOL_WINDOWS = (2, 4, 8, 16)
POOL_GROUP = GROUP_WIDTH // len(POOL_WINDOWS)
D_FF = -((-8 * D_MODEL) // (3 * 256)) * 256
RMS_EPS = 1e-6

IN_SPLIT_SIZES = (GROUP_WIDTH, GROUP_WIDTH, GROUP_WIDTH,
                  GROUP_WIDTH, GROUP_WIDTH, GROUP_WIDTH,
                  GLA_HEADS * GLA_DK, GLA_HEADS * GLA_DK, GROUP_WIDTH, GROUP_WIDTH, GLA_GATE_RANK,
                  GROUP_WIDTH)
IN_COLS = sum(IN_SPLIT_SIZES)
IN_SPLIT_POINTS = tuple(int(v) for v in np.cumsum(IN_SPLIT_SIZES)[:-1])

kernel_name = 'hybrid_parallel_conv_stickbreak_gla_pool_block'


def rms_norm(x, g):
    xf = x.astype(jnp.float32)
    y = xf * lax.rsqrt(jnp.mean(xf * xf, axis=-1, keepdims=True) + RMS_EPS)
    return (y * g.astype(jnp.float32)).astype(x.dtype)


def short_conv_mixer(b, c, h, conv_w):
    S = h.shape[1]
    u = jnp.pad(c * h, ((0, 0), (CONV_K - 1, 0), (0, 0)))
    y = conv_w[0] * u[:, 0:S]
    for i in range(1, CONV_K):
        y = y + conv_w[i] * u[:, i:i + S]
    return b * y


def stick_breaking_mixer(q, k, v, q_g, k_g):
    Bn, S, _ = q.shape
    def heads(t):
        return t.reshape(Bn, S, SB_HEADS, SB_HEAD_DIM).transpose(0, 2, 1, 3)
    qh = rms_norm(heads(q), q_g).astype(jnp.float32)
    kh = rms_norm(heads(k), k_g).astype(jnp.float32)
    vh = heads(v).astype(jnp.float32)
    scale = SB_HEAD_DIM ** -0.5
    outs = []
    for blk in range(S // SB_BLOCK):
        start = blk * SB_BLOCK
        end = start + SB_BLOCK
        kp = kh[:, :, :end]
        vp = vh[:, :, :end]
        z = jnp.einsum('bhtd,bhsd->bhts', qh[:, :, start:end], kp) * scale
        mask = jnp.arange(end)[None, :] < jnp.arange(start, end)[:, None]
        log_rem = jnp.where(mask, jax.nn.log_sigmoid(-z), 0.0)
        csum = jnp.cumsum(log_rem, axis=-1)
        log_a = jax.nn.log_sigmoid(z) + csum[..., -1:] - csum
        a = jnp.where(mask, jnp.exp(log_a), 0.0)
        outs.append(jnp.einsum('bhts,bhsd->bhtd', a, vp))
    o = jnp.concatenate(outs, axis=2)
    return o.transpose(0, 2, 1, 3).reshape(Bn, S, GROUP_WIDTH).astype(q.dtype)


def gla_mixer(q, k, v, r, a_lr, a_w, a_b, norm_g):
    Bn, S, _ = q.shape
    n_chunks = S // CHUNK
    log_f = jax.nn.log_sigmoid((a_lr @ a_w + a_b).astype(jnp.float32)) / GLA_TAU
    def chunks(t, d):
        t = t.astype(jnp.float32).reshape(Bn, n_chunks, CHUNK, GLA_HEADS, d)
        return t.transpose(1, 0, 3, 2, 4)
    qc = chunks(q * GLA_DK ** -0.5, GLA_DK)
    kc = chunks(k, GLA_DK)
    vc = chunks(v, GLA_DV)
    gc = chunks(log_f, GLA_DK)
    causal = jnp.tril(jnp.ones((CHUNK, CHUNK), dtype=bool))[..., None]

    def step(state, inp):
        qi, ki, vi, gi = inp
        b = jnp.cumsum(gi, axis=-2)
        o_inter = jnp.einsum('bhtk,bhkv->bhtv', qi * jnp.exp(b), state)
        diff = b[:, :, :, None, :] - b[:, :, None, :, :]
        decay = jnp.where(causal, jnp.exp(jnp.where(causal, diff, 0.0)), 0.0)
        att = jnp.einsum('bhtk,bhsk,bhtsk->bhts', qi, ki, decay)
        o = o_inter + jnp.einsum('bhts,bhsv->bhtv', att, vi)
        b_last = b[:, :, -1:, :]
        state = (jnp.exp(b_last[:, :, 0, :, None]) * state
                 + jnp.einsum('bhsk,bhsv->bhkv', ki * jnp.exp(b_last - b), vi))
        return state, o

    s0 = jnp.zeros((Bn, GLA_HEADS, GLA_DK, GLA_DV), jnp.float32)
    _, o = lax.scan(step, s0, (qc, kc, vc, gc))
    o = o.transpose(1, 0, 3, 2, 4).reshape(Bn, S, GLA_HEADS, GLA_DV)
    o = rms_norm(o, norm_g).reshape(Bn, S, GROUP_WIDTH)
    o = o * jax.nn.silu(r.astype(jnp.float32))
    return o.astype(q.dtype)


def pool_mixer(u, pool_w, pool_scale):
    Bn, S, _ = u.shape
    uf = u.astype(jnp.float32).reshape(Bn, S, len(POOL_WINDOWS), POOL_GROUP)
    cs = jnp.cumsum(uf, axis=1)
    outs = []
    for g, w in enumerate(POOL_WINDOWS):
        csg = cs[:, :, g]
        prev = jnp.pad(csg[:, :S - w], ((0, 0), (w, 0), (0, 0)))
        count = jnp.minimum(jnp.arange(1, S + 1), w).astype(jnp.float32)[None, :, None]
        outs.append((csg - prev) / count - uf[:, :, g])
    pooled = jnp.stack(outs, axis=2)
    y = jnp.einsum('bsgc,gcd->bsgd', pooled, pool_w.astype(jnp.float32)).reshape(Bn, S, GROUP_WIDTH)
    return (y * pool_scale.astype(jnp.float32)).astype(u.dtype)


def setup_inputs(seed: int = 0) -> dict:
    key = jax.random.key(seed)
    ks = jax.random.split(key, 16)
    f32 = jnp.float32
    def nrm(k, shape, scale):
        return jax.random.normal(k, shape, f32) * scale
    return {
        'x': jax.random.normal(ks[0], (BATCH, SEQ, D_MODEL), f32),
        'norm1_g': 1.0 + nrm(ks[1], (DEPTH, D_MODEL), 0.02),
        'w_in': nrm(ks[2], (DEPTH, D_MODEL, IN_COLS), D_MODEL ** -0.5),
        'conv_w': nrm(ks[3], (DEPTH, CONV_K, GROUP_WIDTH), CONV_K ** -0.5),
        'sb_q_g': 1.0 + nrm(ks[4], (DEPTH, SB_HEAD_DIM), 0.02),
        'sb_k_g': 1.0 + nrm(ks[5], (DEPTH, SB_HEAD_DIM), 0.02),
        'gla_a_w': nrm(ks[6], (DEPTH, GLA_GATE_RANK, GLA_HEADS * GLA_DK), GLA_GATE_RANK ** -0.5),
        'gla_a_b': nrm(ks[7], (DEPTH, GLA_HEADS * GLA_DK), 0.1),
        'gla_norm_g': 1.0 + nrm(ks[8], (DEPTH, GLA_DV), 0.02),
        'pool_w': nrm(ks[9], (DEPTH, len(POOL_WINDOWS), POOL_GROUP, POOL_GROUP), POOL_GROUP ** -0.5),
        'pool_scale': 1.0 + nrm(ks[10], (DEPTH, GROUP_WIDTH), 0.02),
        'w_out': nrm(ks[11], (DEPTH, MIX_WIDTH, D_MODEL), MIX_WIDTH ** -0.5),
        'norm2_g': 1.0 + nrm(ks[12], (DEPTH, D_MODEL), 0.02),
        'w_gate': nrm(ks[13], (DEPTH, D_MODEL, D_FF), D_MODEL ** -0.5),
        'w_up': nrm(ks[14], (DEPTH, D_MODEL, D_FF), D_MODEL ** -0.5),
        'w_down': nrm(ks[15], (DEPTH, D_FF, D_MODEL), D_FF ** -0.5),
    }


def reference(x, norm1_g, w_in, conv_w, sb_q_g, sb_k_g, gla_a_w, gla_a_b, gla_norm_g,
              pool_w, pool_scale, w_out, norm2_g, w_gate, w_up, w_down):
    for l in range(DEPTH):
        h = rms_norm(x, norm1_g[l])
        proj = h @ w_in[l]
        (c_b, c_c, c_h, s_q, s_k, s_v,
         g_q, g_k, g_v, g_r, g_a, p_u) = jnp.split(proj, IN_SPLIT_POINTS, axis=-1)
        y_conv = short_conv_mixer(c_b, c_c, c_h, conv_w[l])
        y_sb = stick_breaking_mixer(s_q, s_k, s_v, sb_q_g[l], sb_k_g[l])
        y_gla = gla_mixer(g_q, g_k, g_v, g_r, g_a, gla_a_w[l], gla_a_b[l], gla_norm_g[l])
        y_pool = pool_mixer(p_u, pool_w[l], pool_scale[l])
        mixed = jnp.concatenate([y_conv, y_sb, y_gla, y_pool], axis=-1)
        x = x + mixed @ w_out[l]
        h = rms_norm(x, norm2_g[l])
        x = x + (jax.nn.silu(h @ w_gate[l]) * (h @ w_up[l])) @ w_down[l]
    return x
```

```python
import functools

import numpy as np
import jax
import jax.numpy as jnp
from jax import lax
from jax.experimental import pallas as pl
from jax.experimental.pallas import tpu as pltpu

F32 = jnp.float32
BF16 = jnp.bfloat16

D_MODEL = 2048
GROUP = D_MODEL // 4
HEAD = 128
N_HEADS = GROUP // HEAD
GLA_DK = 64
GLA_KW = N_HEADS * GLA_DK
GLA_RANK = 16
GLA_TAU = 16.0
CHUNK = 64
N_LEVELS = 6
POOL_WINDOWS = (2, 4, 8, 16)
HALO = 16
D_FF = 5632
EPS = 1e-6
IN_COLS = 5136
PROJ_COLS = 5376
SB_BLOCK = 128

VMEM_LIMIT = 56 * 1024 * 1024


def _cparams(sem):
    return pltpu.CompilerParams(dimension_semantics=sem, vmem_limit_bytes=VMEM_LIMIT)


def _softplus(z):
    return jnp.maximum(z, 0.0) + jnp.log1p(jnp.exp(-jnp.abs(z)))


def _split_bf16(a):
    hi = a.astype(BF16)
    lo = (a - hi.astype(F32)).astype(BF16)
    return hi, lo


def _inproj_kernel(x_ref, g_ref, w_ref, o_ref, h_ref):
    @pl.when(pl.program_id(1) == 0)
    def _():
        x = x_ref[...]
        ms = jnp.mean(x * x, axis=-1, keepdims=True)
        h_ref[...] = (x * lax.rsqrt(ms + EPS) * g_ref[...]).astype(BF16)

    o_ref[...] = jnp.dot(h_ref[...], w_ref[...], preferred_element_type=F32).astype(o_ref.dtype)


def _inproj(x2d, g, w, tm=1024, tn=768):
    t, d = x2d.shape
    n = w.shape[1]
    tm = min(tm, t)
    return pl.pallas_call(
        _inproj_kernel,
        grid=(t // tm, n // tn),
        in_specs=[
            pl.BlockSpec((tm, d), lambda i, j: (i, 0)),
            pl.BlockSpec((1, d), lambda i, j: (0, 0)),
            pl.BlockSpec((d, tn), lambda i, j: (0, j)),
        ],
        out_specs=pl.BlockSpec((tm, tn), lambda i, j: (i, j)),
        out_shape=jax.ShapeDtypeStruct((t, n), BF16),
        scratch_shapes=[pltpu.VMEM((tm, d), BF16)],
        compiler_params=_cparams(("parallel", "arbitrary")),
        name="inproj",
    )(x2d, g, w)


def _convpool_kernel(cb_ref, cc_ref, ch_ref, pu_ref, cc_h_ref, ch_h_ref, pu_h_ref,
                     cw_ref, pw_ref, ps_ref, yc_ref, yp_ref, *, ts):
    s_idx = pl.program_id(1)
    keep = (s_idx > 0).astype(F32)

    def with_halo(h_ref, m_ref):
        return jnp.concatenate([h_ref[...].astype(F32) * keep, m_ref[...].astype(F32)], axis=0)

    def back(a, k):
        return pltpu.roll(a, k, axis=0)

    u = with_halo(cc_h_ref, cc_ref) * with_halo(ch_h_ref, ch_ref)
    cw = cw_ref[...]
    acc = cw[2:3, :] * u + cw[1:2, :] * back(u, 1) + cw[0:1, :] * back(u, 2)
    yc_ref[...] = (cb_ref[...].astype(F32) * acc[HALO:, :]).astype(yc_ref.dtype)

    p = with_halo(pu_h_ref, pu_ref)
    t_glob = s_idx * ts + lax.broadcasted_iota(jnp.int32, (ts, 1), 0)
    for g, w in enumerate(POOL_WINDOWS):
        pg = p[:, g * HEAD:(g + 1) * HEAD]
        win = pg
        k = 1
        while k < w:
            win = win + back(win, k)
            k *= 2
        count = jnp.minimum(t_glob + 1, w).astype(F32)
        pooled = win[HALO:, :] / count - pg[HALO:, :]
        y = jnp.dot(pooled.astype(BF16), pw_ref[g], preferred_element_type=F32)
        yp_ref[:, g * HEAD:(g + 1) * HEAD] = (y * ps_ref[:, g * HEAD:(g + 1) * HEAD]).astype(yp_ref.dtype)


def _convpool(proj, conv_w, pool_w, pool_scale, batch, seq, ts=512):
    t = proj.shape[0]
    ts = min(ts, seq)
    nt = seq // ts
    hb = ts // HALO

    def main(col):
        return pl.BlockSpec((ts, GROUP), lambda b, s: (b * nt + s, col))

    def halo(col):
        return pl.BlockSpec((HALO, GROUP), lambda b, s: (jnp.maximum((b * nt + s) * hb - 1, 0), col))

    out = pl.BlockSpec((ts, GROUP), lambda b, s: (b * nt + s, 0))
    return pl.pallas_call(
        functools.partial(_convpool_kernel, ts=ts),
        grid=(batch, nt),
        in_specs=[main(0), main(1), main(2), main(9), halo(1), halo(2), halo(9),
                  pl.BlockSpec((3, GROUP), lambda b, s: (0, 0)),
                  pl.BlockSpec((len(POOL_WINDOWS), HEAD, HEAD), lambda b, s: (0, 0, 0)),
                  pl.BlockSpec((1, GROUP), lambda b, s: (0, 0))],
        out_specs=[out, out],
        out_shape=[jax.ShapeDtypeStruct((t, GROUP), BF16)] * 2,
        compiler_params=_cparams(("parallel", "arbitrary")),
        name="convpool",
    )(proj, proj, proj, proj, proj, proj, proj, conv_w, pool_w, pool_scale)


def _sb_kernel(q_ref, k_ref, v_ref, qg_ref, kg_ref, m2_ref, o_ref, qn_ref, kn_ref, *, seq):
    def norm(a, g):
        a = a.astype(F32)
        return a * lax.rsqrt(jnp.mean(a * a, axis=-1, keepdims=True) + EPS) * g

    scale = HEAD ** -0.5
    qn_ref[...] = (norm(q_ref[...], qg_ref[...]) * scale).astype(BF16)
    kn_ref[...] = norm(k_ref[...], kg_ref[...]).astype(BF16)
    m2 = m2_ref[...]

    row = lax.broadcasted_iota(jnp.int32, (SB_BLOCK, SB_BLOCK), 0)
    col = lax.broadcasted_iota(jnp.int32, (SB_BLOCK, SB_BLOCK), 1)
    strict = col < row

    def block(qb, j, rem, acc, mask):
        r0 = pl.multiple_of(j * SB_BLOCK, SB_BLOCK)
        kb = kn_ref[pl.ds(r0, SB_BLOCK), :]
        vb = v_ref[pl.ds(r0, SB_BLOCK), :]
        z = lax.dot_general(qb, kb, (((1,), (1,)), ((), ())), preferred_element_type=F32)
        log_rem = -_softplus(z)
        log_beta = z + log_rem
        if mask is not None:
            log_rem = jnp.where(mask, log_rem, 0.0)
        hi, lo = _split_bf16(log_rem)
        sums = jnp.dot(jnp.concatenate([hi, lo], axis=1), m2, preferred_element_type=F32)
        inside = sums[:, :SB_BLOCK]
        total = sums[:, SB_BLOCK:]
        a = jnp.exp(log_beta + inside + rem)
        if mask is not None:
            a = jnp.where(mask, a, 0.0)
        acc = acc + jnp.dot(a.astype(BF16), vb, preferred_element_type=F32)
        return rem + total, acc

    def q_block(i, carry):
        q0 = pl.multiple_of(i * SB_BLOCK, SB_BLOCK)
        qb = qn_ref[pl.ds(q0, SB_BLOCK), :]
        zero = jnp.zeros((SB_BLOCK, SB_BLOCK), F32)
        rem, acc = block(qb, i, zero, zero, strict)

        def inner(step, c):
            return block(qb, i - 1 - step, c[0], c[1], None)

        rem, acc = lax.fori_loop(0, i, inner, (rem, acc))
        o_ref[pl.ds(q0, SB_BLOCK), :] = acc.astype(o_ref.dtype)
        return carry

    lax.fori_loop(0, seq // SB_BLOCK, q_block, 0)


def _sb_consts():
    j = np.arange(SB_BLOCK)[:, None]
    s = np.arange(SB_BLOCK)[None, :]
    half = np.concatenate([(j > s).astype(np.float32), np.ones((SB_BLOCK, SB_BLOCK), np.float32)], axis=1)
    return jnp.asarray(np.concatenate([half, half], axis=0), BF16)


def _stickbreak(proj, q_g, k_g, batch, seq):
    t = proj.shape[0]
    q0, k0, v0 = 3 * N_HEADS, 4 * N_HEADS, 5 * N_HEADS

    def col(c0):
        return pl.BlockSpec((seq, HEAD), lambda b, h: (b, c0 + h))

    const = lambda shape: pl.BlockSpec(shape, lambda b, h: (0, 0))
    return pl.pallas_call(
        functools.partial(_sb_kernel, seq=seq),
        grid=(batch, N_HEADS),
        in_specs=[col(q0), col(k0), col(v0), const((1, HEAD)), const((1, HEAD)),
                  const((2 * SB_BLOCK, 2 * SB_BLOCK))],
        out_specs=pl.BlockSpec((seq, HEAD), lambda b, h: (b, h)),
        out_shape=jax.ShapeDtypeStruct((t, GROUP), BF16),
        scratch_shapes=[pltpu.VMEM((seq, HEAD), BF16), pltpu.VMEM((seq, HEAD), BF16)],
        compiler_params=_cparams(("parallel", "parallel")),
        name="stickbreak",
    )(proj, proj, proj, q_g, k_g, _sb_consts())


def _gla_consts():
    c = CHUNK
    t = np.arange(c)[:, None]
    j = np.arange(c)[None, :]
    blocks = [(j <= t)]
    qf, kf, masks = [], [], []
    for l in range(N_LEVELS):
        n = 1 << l
        mid = (t // (2 * n)) * (2 * n) + n - 1
        odd = (t // n) % 2 == 1
        qf.append(odd & (j > mid) & (j <= t))
        kf.append((~odd) & (j > t) & (j <= mid))
        tt, ss = np.arange(c)[:, None], np.arange(c)[None, :]
        masks.append((tt // (2 * n) == ss // (2 * n)) & ((tt // n) % 2 == 1) & ((ss // n) % 2 == 0))
    masks.append(np.eye(c, dtype=bool))
    blocks += qf + kf + [(j > t)]
    a = np.concatenate(blocks, axis=0).astype(np.float32)
    a2 = np.concatenate([a, a], axis=1)
    m = np.stack(masks).astype(np.float32)
    m = np.tile(m, (1, N_HEADS, N_HEADS))
    return jnp.asarray(a2, BF16), jnp.asarray(m, F32)


def _gla_kernel(q_ref, k_ref, v_ref, r_ref, a_ref, aw_ref, ab_ref, ng_ref, a2_ref, m_ref,
                o_ref, st_ref, *, seq):
    c = CHUNK
    st_ref[...] = jnp.zeros_like(st_ref)
    lane_head = lax.broadcasted_iota(jnp.int32, (1, GLA_KW), 1) // GLA_DK
    nt = (((1,), (1,)), ((), ()))
    tn = (((0,), (0,)), ((), ()))

    def stack_heads(a):
        return jnp.concatenate([jnp.where(lane_head == h, a, 0.0) for h in range(N_HEADS)], axis=0)

    def tile_rows(a):
        return jnp.concatenate([a] * N_HEADS, axis=0)

    def chunk(ci, carry):
        r0 = pl.multiple_of(ci * c, c)
        rows = pl.ds(r0, c)
        xg = jnp.dot(a_ref[rows, :], aw_ref[...], preferred_element_type=F32) + ab_ref[...]
        g = -_softplus(-xg) * (1.0 / GLA_TAU)
        g_hi, g_lo = _split_bf16(g)
        e = jnp.exp(jnp.dot(a2_ref[...], jnp.concatenate([g_hi, g_lo], axis=0),
                            preferred_element_type=F32))
        eb = e[0:c]
        ekd = e[(1 + 2 * N_LEVELS) * c:(2 + 2 * N_LEVELS) * c]

        q = q_ref[rows, :].astype(F32) * (GLA_DK ** -0.5)
        k = k_ref[rows, :].astype(F32)
        v = v_ref[rows, :]
        v_cat = jnp.concatenate([v[:, h * HEAD:(h + 1) * HEAD] for h in range(N_HEADS)], axis=0)
        qs = stack_heads(q)
        ks = stack_heads(k)

        att = m_ref[N_LEVELS] * lax.dot_general(qs.astype(BF16), ks.astype(BF16), nt,
                                                preferred_element_type=F32)
        for l in range(N_LEVELS):
            eq = tile_rows(e[(1 + l) * c:(2 + l) * c])
            ek = tile_rows(e[(1 + N_LEVELS + l) * c:(2 + N_LEVELS + l) * c])
            p = lax.dot_general((qs * eq).astype(BF16), (ks * ek).astype(BF16), nt,
                                preferred_element_type=F32)
            att = att + m_ref[l] * p
        o = jnp.dot(att.astype(BF16), v_cat, preferred_element_type=F32)

        st = st_ref[...]
        o = o + lax.dot_general((qs * tile_rows(eb)).astype(BF16), st.astype(BF16), nt,
                                preferred_element_type=F32)
        upd = lax.dot_general(v_cat, (ks * tile_rows(ekd)).astype(BF16), tn,
                              preferred_element_type=F32)
        st_ref[...] = st * eb[c - 1:c, :] + upd

        o = o * lax.rsqrt(jnp.mean(o * o, axis=-1, keepdims=True) + EPS) * ng_ref[...]
        o = jnp.concatenate([o[h * c:(h + 1) * c] for h in range(N_HEADS)], axis=1)
        r = r_ref[rows, :].astype(F32)
        o_ref[rows, :] = (o * (r * jax.nn.sigmoid(r))).astype(o_ref.dtype)
        return carry

    lax.fori_loop(0, seq // c, chunk, 0)


def _gla(proj, a_w, a_b, norm_g, batch, seq):
    t = proj.shape[0]
    a2, masks = _gla_consts()

    def col(width, idx):
        return pl.BlockSpec((seq, width), lambda b: (b, idx))

    def const(shape):
        return pl.BlockSpec(shape, lambda b: (0,) * len(shape))

    return pl.pallas_call(
        functools.partial(_gla_kernel, seq=seq),
        grid=(batch,),
        in_specs=[col(GLA_KW, 12), col(GLA_KW, 13), col(GROUP, 7), col(GROUP, 8), col(HEAD, 40),
                  const(a_w.shape), const(a_b.shape), const(norm_g.shape),
                  const(a2.shape), const(masks.shape)],
        out_specs=pl.BlockSpec((seq, GROUP), lambda b: (b, 0)),
        out_shape=jax.ShapeDtypeStruct((t, GROUP), BF16),
        scratch_shapes=[pltpu.VMEM((HEAD, GLA_KW), F32)],
        compiler_params=_cparams(("parallel",)),
        name="gla",
    )(proj, proj, proj, proj, proj, a_w, a_b, norm_g, a2, masks)


def _outproj_kernel(x_ref, yc_ref, ys_ref, yg_ref, yp_ref, w_ref, o_ref):
    mixed = jnp.concatenate([yc_ref[...], ys_ref[...], yg_ref[...], yp_ref[...]], axis=1)
    o_ref[...] = x_ref[...] + jnp.dot(mixed, w_ref[...], preferred_element_type=F32)


def _outproj(x2d, ys, w, tm=512):
    t, d = x2d.shape
    tm = min(tm, t)
    row = lambda width: pl.BlockSpec((tm, width), lambda i: (i, 0))
    return pl.pallas_call(
        _outproj_kernel,
        grid=(t // tm,),
        in_specs=[row(d)] + [row(GROUP)] * 4 + [pl.BlockSpec((d, d), lambda i: (0, 0))],
        out_specs=row(d),
        out_shape=jax.ShapeDtypeStruct((t, d), F32),
        compiler_params=_cparams(("parallel",)),
        name="outproj",
    )(x2d, *ys, w)


def _ffn_kernel(x_ref, g_ref, wg_ref, wu_ref, wd_ref, o_ref, h_ref):
    @pl.when(pl.program_id(1) == 0)
    def _():
        x = x_ref[...]
        ms = jnp.mean(x * x, axis=-1, keepdims=True)
        h_ref[...] = (x * lax.rsqrt(ms + EPS) * g_ref[...]).astype(BF16)
        o_ref[...] = x

    h = h_ref[...]
    gate = jnp.dot(h, wg_ref[...], preferred_element_type=F32)
    up = jnp.dot(h, wu_ref[...], preferred_element_type=F32)
    act = (gate * jax.nn.sigmoid(gate) * up).astype(BF16)
    o_ref[...] += jnp.dot(act, wd_ref[...], preferred_element_type=F32)


def _ffn(x2d, g, wg, wu, wd, tm=512, tf=512):
    t, d = x2d.shape
    f = wg.shape[1]
    tm = min(tm, t)
    return pl.pallas_call(
        _ffn_kernel,
        grid=(t // tm, f // tf),
        in_specs=[
            pl.BlockSpec((tm, d), lambda i, j: (i, 0)),
            pl.BlockSpec((1, d), lambda i, j: (0, 0)),
            pl.BlockSpec((d, tf), lambda i, j: (0, j)),
            pl.BlockSpec((d, tf), lambda i, j: (0, j)),
            pl.BlockSpec((tf, d), lambda i, j: (j, 0)),
        ],
        out_specs=pl.BlockSpec((tm, d), lambda i, j: (i, 0)),
        out_shape=jax.ShapeDtypeStruct((t, d), F32),
        scratch_shapes=[pltpu.VMEM((tm, d), BF16)],
        compiler_params=_cparams(("parallel", "arbitrary")),
        name="ffn",
    )(x2d, g, wg, wu, wd)


def _prep_w_in(w):
    d = w.shape[0]
    ga0 = IN_COLS - GROUP - GLA_RANK
    pad = jnp.zeros((d, PROJ_COLS - IN_COLS), w.dtype)
    return jnp.concatenate([w[:, :ga0], w[:, ga0 + GLA_RANK:], w[:, ga0:ga0 + GLA_RANK], pad], axis=1).astype(BF16)


def kernel(x, norm1_g, w_in, conv_w, sb_q_g, sb_k_g, gla_a_w, gla_a_b, gla_norm_g, pool_w, pool_scale,
           w_out, norm2_g, w_gate, w_up, w_down):
    batch, seq, d = x.shape
    depth = w_in.shape[0]
    x2d = x.reshape(batch * seq, d)
    for l in range(depth):
        proj = _inproj(x2d, norm1_g[l][None, :], _prep_w_in(w_in[l]))
        y_conv, y_pool = _convpool(proj, conv_w[l], pool_w[l].astype(BF16), pool_scale[l][None, :], batch, seq)
        y_sb = _stickbreak(proj, sb_q_g[l][None, :], sb_k_g[l][None, :], batch, seq)
        a_w = jnp.pad(gla_a_w[l], ((0, HEAD - GLA_RANK), (0, 0))).astype(BF16)
        y_gla = _gla(proj, a_w, gla_a_b[l][None, :], gla_norm_g[l][None, :], batch, seq)
        x2d = _outproj(x2d, (y_conv, y_sb, y_gla, y_pool), w_out[l].astype(BF16))
        x2d = _ffn(x2d, norm2_g[l][None, :], w_gate[l].astype(BF16), w_up[l].astype(BF16), w_down[l].astype(BF16))
    return x2d.reshape(batch, seq, d)
```

```python
import functools

import numpy as np
import jax
import jax.numpy as jnp
from jax import lax
from jax.experimental import pallas as pl
from jax.experimental.pallas import tpu as pltpu

F32 = jnp.float32
BF16 = jnp.bfloat16

D_MODEL = 2048
GROUP = D_MODEL // 4
HEAD = 128
N_HEADS = GROUP // HEAD
GLA_DK = 64
GLA_KW = N_HEADS * GLA_DK
GLA_RANK = 16
GLA_TAU = 16.0
CHUNK = 64
N_LEVELS = 6
POOL_WINDOWS = (2, 4, 8, 16)
HALO = 16
D_FF = 5632
EPS = 1e-6
IN_COLS = 5136
PROJ_COLS = 5376
SB_BLOCK = 256

VMEM_LIMIT = 56 * 1024 * 1024


def _cparams(sem):
    return pltpu.CompilerParams(dimension_semantics=sem, vmem_limit_bytes=VMEM_LIMIT)


def _softplus(z):
    return jnp.maximum(z, 0.0) + jnp.log(1.0 + jnp.exp(-jnp.abs(z)))


def _split_bf16(a):
    hi = a.astype(BF16)
    lo = (a - hi.astype(F32)).astype(BF16)
    return hi, lo


def _inproj_kernel(x_ref, g_ref, w_ref, o_ref, h_ref):
    @pl.when(pl.program_id(1) == 0)
    def _():
        x = x_ref[...]
        ms = jnp.mean(x * x, axis=-1, keepdims=True)
        h_ref[...] = (x * lax.rsqrt(ms + EPS) * g_ref[...]).astype(BF16)

    o_ref[...] = jnp.dot(h_ref[...], w_ref[...], preferred_element_type=F32).astype(o_ref.dtype)


def _inproj(x2d, g, w, tm=1024, tn=768):
    t, d = x2d.shape
    n = w.shape[1]
    tm = min(tm, t)
    return pl.pallas_call(
        _inproj_kernel,
        grid=(t // tm, n // tn),
        in_specs=[
            pl.BlockSpec((tm, d), lambda i, j: (i, 0)),
            pl.BlockSpec((1, d), lambda i, j: (0, 0)),
            pl.BlockSpec((d, tn), lambda i, j: (0, j)),
        ],
        out_specs=pl.BlockSpec((tm, tn), lambda i, j: (i, j)),
        out_shape=jax.ShapeDtypeStruct((t, n), BF16),
        scratch_shapes=[pltpu.VMEM((tm, d), BF16)],
        compiler_params=_cparams(("parallel", "arbitrary")),
        name="inproj",
    )(x2d, g, w)


def _convpool_kernel(cb_ref, cc_ref, ch_ref, pu_ref, cc_h_ref, ch_h_ref, pu_h_ref,
                     cw_ref, pw_ref, ps_ref, yc_ref, yp_ref, *, ts):
    s_idx = pl.program_id(1)
    keep = (s_idx > 0).astype(F32)

    def with_halo(h_ref, m_ref):
        return jnp.concatenate([h_ref[...].astype(F32) * keep, m_ref[...].astype(F32)], axis=0)

    def back(a, k):
        return pltpu.roll(a, k, axis=0)

    u = with_halo(cc_h_ref, cc_ref) * with_halo(ch_h_ref, ch_ref)
    cw = cw_ref[...]
    acc = cw[2:3, :] * u + cw[1:2, :] * back(u, 1) + cw[0:1, :] * back(u, 2)
    yc_ref[...] = (cb_ref[...].astype(F32) * acc[HALO:, :]).astype(yc_ref.dtype)

    p = with_halo(pu_h_ref, pu_ref)
    t_glob = s_idx * ts + lax.broadcasted_iota(jnp.int32, (ts, 1), 0)
    for g, w in enumerate(POOL_WINDOWS):
        pg = p[:, g * HEAD:(g + 1) * HEAD]
        win = pg
        k = 1
        while k < w:
            win = win + back(win, k)
            k *= 2
        count = jnp.minimum(t_glob + 1, w).astype(F32)
        pooled = win[HALO:, :] / count - pg[HALO:, :]
        y = jnp.dot(pooled.astype(BF16), pw_ref[g], preferred_element_type=F32)
        yp_ref[:, g * HEAD:(g + 1) * HEAD] = (y * ps_ref[:, g * HEAD:(g + 1) * HEAD]).astype(yp_ref.dtype)


def _convpool(proj, conv_w, pool_w, pool_scale, batch, seq, ts=512):
    t = proj.shape[0]
    ts = min(ts, seq)
    nt = seq // ts
    hb = ts // HALO

    def main(col):
        return pl.BlockSpec((ts, GROUP), lambda b, s: (b * nt + s, col))

    def halo(col):
        return pl.BlockSpec((HALO, GROUP), lambda b, s: (jnp.maximum((b * nt + s) * hb - 1, 0), col))

    out = pl.BlockSpec((ts, GROUP), lambda b, s: (b * nt + s, 0))
    return pl.pallas_call(
        functools.partial(_convpool_kernel, ts=ts),
        grid=(batch, nt),
        in_specs=[main(0), main(1), main(2), main(9), halo(1), halo(2), halo(9),
                  pl.BlockSpec((3, GROUP), lambda b, s: (0, 0)),
                  pl.BlockSpec((len(POOL_WINDOWS), HEAD, HEAD), lambda b, s: (0, 0, 0)),
                  pl.BlockSpec((1, GROUP), lambda b, s: (0, 0))],
        out_specs=[out, out],
        out_shape=[jax.ShapeDtypeStruct((t, GROUP), BF16)] * 2,
        compiler_params=_cparams(("parallel", "arbitrary")),
        name="convpool",
    )(proj, proj, proj, proj, proj, proj, proj, conv_w, pool_w, pool_scale)


def _sb_kernel(q_ref, k_ref, v_ref, qg_ref, kg_ref, m2_ref, o_ref, qn_ref, kn_ref, rem_ref, acc_ref, *, seq):
    def norm(a, g):
        a = a.astype(F32)
        return a * lax.rsqrt(jnp.mean(a * a, axis=-1, keepdims=True) + EPS) * g

    scale = HEAD ** -0.5
    qn_ref[...] = (norm(q_ref[...], qg_ref[...]) * scale).astype(BF16)
    kn_ref[...] = norm(k_ref[...], kg_ref[...]).astype(BF16)
    nq = seq // SB_BLOCK

    def tile(qi, d):
        q0 = qi * SB_BLOCK
        k0 = (qi - d) * SB_BLOCK
        qb = qn_ref[pl.ds(q0, SB_BLOCK), :]
        kb = kn_ref[pl.ds(k0, SB_BLOCK), :]
        vb = v_ref[pl.ds(k0, SB_BLOCK), :]
        z = lax.dot_general(qb, kb, (((1,), (1,)), ((), ())), preferred_element_type=F32)
        log_rem = -_softplus(z)
        log_a = z + log_rem
        if d == 0:
            row = lax.broadcasted_iota(jnp.int32, (SB_BLOCK, SB_BLOCK), 0)
            col = lax.broadcasted_iota(jnp.int32, (SB_BLOCK, SB_BLOCK), 1)
            strict = col < row
            log_rem = jnp.where(strict, log_rem, 0.0)
        hi, lo = _split_bf16(log_rem)
        log_a = log_a + jnp.dot(jnp.concatenate([hi, lo], axis=1), m2_ref[...], preferred_element_type=F32)
        total = jnp.sum(log_rem, axis=1, keepdims=True)
        if d == 0:
            a = jnp.where(strict, jnp.exp(log_a), 0.0)
            acc_ref[pl.ds(q0, SB_BLOCK), :] = jnp.dot(a.astype(BF16), vb, preferred_element_type=F32)
            rem_ref[pl.ds(q0, SB_BLOCK), :] = jnp.broadcast_to(total, (SB_BLOCK, HEAD))
        else:
            rem = rem_ref[pl.ds(q0, SB_BLOCK), :]
            a = jnp.exp(log_a + jnp.concatenate([rem] * (SB_BLOCK // HEAD), axis=1))
            acc_ref[pl.ds(q0, SB_BLOCK), :] += jnp.dot(a.astype(BF16), vb, preferred_element_type=F32)
            rem_ref[pl.ds(q0, SB_BLOCK), :] = rem + total

    for d in range(nq):
        for qi in range(d, nq):
            tile(qi, d)

    o_ref[...] = acc_ref[...].astype(o_ref.dtype)


def _sb_consts():
    j = np.arange(SB_BLOCK)[:, None]
    s = np.arange(SB_BLOCK)[None, :]
    m = (j > s).astype(np.float32)
    return jnp.asarray(np.concatenate([m, m], axis=0), BF16)


def _stickbreak(proj, q_g, k_g, batch, seq):
    t = proj.shape[0]
    q0, k0, v0 = 3 * N_HEADS, 4 * N_HEADS, 5 * N_HEADS

    def col(c0):
        return pl.BlockSpec((seq, HEAD), lambda b, h: (b, c0 + h))

    const = lambda shape: pl.BlockSpec(shape, lambda b, h: (0, 0))
    return pl.pallas_call(
        functools.partial(_sb_kernel, seq=seq),
        grid=(batch, N_HEADS),
        in_specs=[col(q0), col(k0), col(v0), const((1, HEAD)), const((1, HEAD)),
                  const((2 * SB_BLOCK, SB_BLOCK))],
        out_specs=pl.BlockSpec((seq, HEAD), lambda b, h: (b, h)),
        out_shape=jax.ShapeDtypeStruct((t, GROUP), BF16),
        scratch_shapes=[pltpu.VMEM((seq, HEAD), BF16), pltpu.VMEM((seq, HEAD), BF16),
                        pltpu.VMEM((seq, HEAD), F32), pltpu.VMEM((seq, HEAD), F32)],
        compiler_params=_cparams(("parallel", "parallel")),
        name="stickbreak",
    )(proj, proj, proj, q_g, k_g, _sb_consts())


def _gla_consts():
    c = CHUNK
    t = np.arange(c)[:, None]
    j = np.arange(c)[None, :]
    blocks = [(j <= t)]
    qf, kf, masks = [], [], []
    for l in range(N_LEVELS):
        n = 1 << l
        mid = (t // (2 * n)) * (2 * n) + n - 1
        odd = (t // n) % 2 == 1
        qf.append(odd & (j > mid) & (j <= t))
        kf.append((~odd) & (j > t) & (j <= mid))
        tt, ss = np.arange(c)[:, None], np.arange(c)[None, :]
        masks.append((tt // (2 * n) == ss // (2 * n)) & ((tt // n) % 2 == 1) & ((ss // n) % 2 == 0))
    masks.append(np.eye(c, dtype=bool))
    blocks += qf + kf + [(j > t)]
    a = np.concatenate(blocks, axis=0).astype(np.float32)
    a2 = np.concatenate([a, a], axis=1)
    m = np.stack(masks).astype(np.float32)
    m = np.tile(m, (1, N_HEADS, N_HEADS))
    return jnp.asarray(a2, BF16), jnp.asarray(m, F32)


def _gla_kernel(q_ref, k_ref, v_ref, r_ref, a_ref, aw_ref, ab_ref, ng_ref, a2_ref, m_ref,
                o_ref, st_ref, *, seq):
    c = CHUNK
    st_ref[...] = jnp.zeros_like(st_ref)
    lane_head = lax.broadcasted_iota(jnp.int32, (1, GLA_KW), 1) // GLA_DK
    nt = (((1,), (1,)), ((), ()))
    tn = (((0,), (0,)), ((), ()))

    def stack_heads(a):
        return jnp.concatenate([jnp.where(lane_head == h, a, 0.0) for h in range(N_HEADS)], axis=0)

    def tile_rows(a):
        return jnp.concatenate([a] * N_HEADS, axis=0)

    def chunk(ci, carry):
        r0 = pl.multiple_of(ci * c, c)
        rows = pl.ds(r0, c)
        xg = jnp.dot(a_ref[rows, :], aw_ref[...], preferred_element_type=F32) + ab_ref[...]
        g = -_softplus(-xg) * (1.0 / GLA_TAU)
        g_hi, g_lo = _split_bf16(g)
        e = jnp.exp(jnp.dot(a2_ref[...], jnp.concatenate([g_hi, g_lo], axis=0),
                            preferred_element_type=F32))
        eb = e[0:c]
        ekd = e[(1 + 2 * N_LEVELS) * c:(2 + 2 * N_LEVELS) * c]

        q = q_ref[rows, :].astype(F32) * (GLA_DK ** -0.5)
        k = k_ref[rows, :].astype(F32)
        v = v_ref[rows, :]
        v_cat = jnp.concatenate([v[:, h * HEAD:(h + 1) * HEAD] for h in range(N_HEADS)], axis=0)
        qs = stack_heads(q)
        ks = stack_heads(k)

        att = m_ref[N_LEVELS] * lax.dot_general(qs.astype(BF16), ks.astype(BF16), nt,
                                                preferred_element_type=F32)
        for l in range(N_LEVELS):
            eq = tile_rows(e[(1 + l) * c:(2 + l) * c])
            ek = tile_rows(e[(1 + N_LEVELS + l) * c:(2 + N_LEVELS + l) * c])
            p = lax.dot_general((qs * eq).astype(BF16), (ks * ek).astype(BF16), nt,
                                preferred_element_type=F32)
            att = att + m_ref[l] * p
        o = jnp.dot(att.astype(BF16), v_cat, preferred_element_type=F32)

        st = st_ref[...]
        o = o + lax.dot_general((qs * tile_rows(eb)).astype(BF16), st.astype(BF16), nt,
                                preferred_element_type=F32)
        upd = lax.dot_general(v_cat, (ks * tile_rows(ekd)).astype(BF16), tn,
                              preferred_element_type=F32)
        st_ref[...] = st * eb[c - 1:c, :] + upd

        o = o * lax.rsqrt(jnp.mean(o * o, axis=-1, keepdims=True) + EPS) * ng_ref[...]
        o = jnp.concatenate([o[h * c:(h + 1) * c] for h in range(N_HEADS)], axis=1)
        r = r_ref[rows, :].astype(F32)
        o_ref[rows, :] = (o * (r * jax.nn.sigmoid(r))).astype(o_ref.dtype)
        return carry

    lax.fori_loop(0, seq // c, chunk, 0)


def _gla(proj, a_w, a_b, norm_g, batch, seq):
    t = proj.shape[0]
    a2, masks = _gla_consts()

    def col(width, idx):
        return pl.BlockSpec((seq, width), lambda b: (b, idx))

    def const(shape):
        return pl.BlockSpec(shape, lambda b: (0,) * len(shape))

    return pl.pallas_call(
        functools.partial(_gla_kernel, seq=seq),
        grid=(batch,),
        in_specs=[col(GLA_KW, 12), col(GLA_KW, 13), col(GROUP, 7), col(GROUP, 8), col(HEAD, 40),
                  const(a_w.shape), const(a_b.shape), const(norm_g.shape),
                  const(a2.shape), const(masks.shape)],
        out_specs=pl.BlockSpec((seq, GROUP), lambda b: (b, 0)),
        out_shape=jax.ShapeDtypeStruct((t, GROUP), BF16),
        scratch_shapes=[pltpu.VMEM((HEAD, GLA_KW), F32)],
        compiler_params=_cparams(("parallel",)),
        name="gla",
    )(proj, proj, proj, proj, proj, a_w, a_b, norm_g, a2, masks)


def _outproj_kernel(x_ref, yc_ref, ys_ref, yg_ref, yp_ref, w_ref, o_ref):
    mixed = jnp.concatenate([yc_ref[...], ys_ref[...], yg_ref[...], yp_ref[...]], axis=1)
    o_ref[...] = x_ref[...] + jnp.dot(mixed, w_ref[...], preferred_element_type=F32)


def _outproj(x2d, ys, w, tm=512):
    t, d = x2d.shape
    tm = min(tm, t)
    row = lambda width: pl.BlockSpec((tm, width), lambda i: (i, 0))
    return pl.pallas_call(
        _outproj_kernel,
        grid=(t // tm,),
        in_specs=[row(d)] + [row(GROUP)] * 4 + [pl.BlockSpec((d, d), lambda i: (0, 0))],
        out_specs=row(d),
        out_shape=jax.ShapeDtypeStruct((t, d), F32),
        compiler_params=_cparams(("parallel",)),
        name="outproj",
    )(x2d, *ys, w)


def _ffn_kernel(x_ref, g_ref, wg_ref, wu_ref, wd_ref, o_ref, h_ref):
    @pl.when(pl.program_id(1) == 0)
    def _():
        x = x_ref[...]
        ms = jnp.mean(x * x, axis=-1, keepdims=True)
        h_ref[...] = (x * lax.rsqrt(ms + EPS) * g_ref[...]).astype(BF16)
        o_ref[...] = x

    h = h_ref[...]
    gate = jnp.dot(h, wg_ref[...], preferred_element_type=F32)
    up = jnp.dot(h, wu_ref[...], preferred_element_type=F32)
    act = (gate * jax.nn.sigmoid(gate) * up).astype(BF16)
    o_ref[...] += jnp.dot(act, wd_ref[...], preferred_element_type=F32)


def _ffn(x2d, g, wg, wu, wd, tm=512, tf=512):
    t, d = x2d.shape
    f = wg.shape[1]
    tm = min(tm, t)
    return pl.pallas_call(
        _ffn_kernel,
        grid=(t // tm, f // tf),
        in_specs=[
            pl.BlockSpec((tm, d), lambda i, j: (i, 0)),
            pl.BlockSpec((1, d), lambda i, j: (0, 0)),
            pl.BlockSpec((d, tf), lambda i, j: (0, j)),
            pl.BlockSpec((d, tf), lambda i, j: (0, j)),
            pl.BlockSpec((tf, d), lambda i, j: (j, 0)),
        ],
        out_specs=pl.BlockSpec((tm, d), lambda i, j: (i, 0)),
        out_shape=jax.ShapeDtypeStruct((t, d), F32),
        scratch_shapes=[pltpu.VMEM((tm, d), BF16)],
        compiler_params=_cparams(("parallel", "arbitrary")),
        name="ffn",
    )(x2d, g, wg, wu, wd)


def _prep_w_in(w):
    d = w.shape[0]
    ga0 = IN_COLS - GROUP - GLA_RANK
    pad = jnp.zeros((d, PROJ_COLS - IN_COLS), w.dtype)
    return jnp.concatenate([w[:, :ga0], w[:, ga0 + GLA_RANK:], w[:, ga0:ga0 + GLA_RANK], pad], axis=1).astype(BF16)


def kernel(x, norm1_g, w_in, conv_w, sb_q_g, sb_k_g, gla_a_w, gla_a_b, gla_norm_g, pool_w, pool_scale,
           w_out, norm2_g, w_gate, w_up, w_down):
    batch, seq, d = x.shape
    depth = w_in.shape[0]
    x2d = x.reshape(batch * seq, d)
    for l in range(depth):
        proj = _inproj(x2d, norm1_g[l][None, :], _prep_w_in(w_in[l]))
        y_conv, y_pool = _convpool(proj, conv_w[l], pool_w[l].astype(BF16), pool_scale[l][None, :], batch, seq)
        y_sb = _stickbreak(proj, sb_q_g[l][None, :], sb_k_g[l][None, :], batch, seq)
        a_w = jnp.pad(gla_a_w[l], ((0, HEAD - GLA_RANK), (0, 0))).astype(BF16)
        y_gla = _gla(proj, a_w, gla_a_b[l][None, :], gla_norm_g[l][None, :], batch, seq)
        x2d = _outproj(x2d, (y_conv, y_sb, y_gla, y_pool), w_out[l].astype(BF16))
        x2d = _ffn(x2d, norm2_g[l][None, :], w_gate[l].astype(BF16), w_up[l].astype(BF16), w_down[l].astype(BF16))
    return x2d.reshape(batch, seq, d)
```

```python
import functools

import numpy as np
import jax
import jax.numpy as jnp
from jax import lax
from jax.experimental import pallas as pl
from jax.experimental.pallas import tpu as pltpu

F32 = jnp.float32
BF16 = jnp.bfloat16

D_MODEL = 2048
GROUP = D_MODEL // 4
HEAD = 128
N_HEADS = GROUP // HEAD
GLA_DK = 64
GLA_KW = N_HEADS * GLA_DK
GLA_RANK = 16
GLA_TAU = 16.0
CHUNK = 64
N_LEVELS = 6
POOL_WINDOWS = (2, 4, 8, 16)
HALO = 16
D_FF = 5632
EPS = 1e-6
IN_COLS = 5136
PROJ_COLS = 5376
SB_BLOCK = 256

VMEM_LIMIT = 56 * 1024 * 1024


def _cparams(sem):
    return pltpu.CompilerParams(dimension_semantics=sem, vmem_limit_bytes=VMEM_LIMIT)


def _softplus(z):
    return jnp.maximum(z, 0.0) + jnp.log(1.0 + jnp.exp(-jnp.abs(z)))


def _split_bf16(a):
    hi = a.astype(BF16)
    lo = (a - hi.astype(F32)).astype(BF16)
    return hi, lo


def _inproj_kernel(x_ref, g_ref, w_ref, o_ref, h_ref):
    @pl.when(pl.program_id(1) == 0)
    def _():
        x = x_ref[...]
        ms = jnp.mean(x * x, axis=-1, keepdims=True)
        h_ref[...] = (x * lax.rsqrt(ms + EPS) * g_ref[...]).astype(BF16)

    o_ref[...] = jnp.dot(h_ref[...], w_ref[...], preferred_element_type=F32).astype(o_ref.dtype)


def _inproj(x2d, g, w, tm=1024, tn=1792):
    t, d = x2d.shape
    n = w.shape[1]
    tm = min(tm, t)
    return pl.pallas_call(
        _inproj_kernel,
        grid=(t // tm, n // tn),
        in_specs=[
            pl.BlockSpec((tm, d), lambda i, j: (i, 0)),
            pl.BlockSpec((1, d), lambda i, j: (0, 0)),
            pl.BlockSpec((d, tn), lambda i, j: (0, j)),
        ],
        out_specs=pl.BlockSpec((tm, tn), lambda i, j: (i, j)),
        out_shape=jax.ShapeDtypeStruct((t, n), BF16),
        scratch_shapes=[pltpu.VMEM((tm, d), BF16)],
        compiler_params=_cparams(("parallel", "arbitrary")),
        name="inproj",
    )(x2d, g, w)


def _convpool_kernel(cb_ref, cc_ref, ch_ref, pu_ref, cc_h_ref, ch_h_ref, pu_h_ref,
                     cw_ref, pw_ref, ps_ref, yc_ref, yp_ref, *, ts):
    s_idx = pl.program_id(1)
    keep = (s_idx > 0).astype(F32)

    def with_halo(h_ref, m_ref):
        return jnp.concatenate([h_ref[...].astype(F32) * keep, m_ref[...].astype(F32)], axis=0)

    def back(a, k):
        return pltpu.roll(a, k, axis=0)

    u = with_halo(cc_h_ref, cc_ref) * with_halo(ch_h_ref, ch_ref)
    cw = cw_ref[...]
    acc = cw[2:3, :] * u + cw[1:2, :] * back(u, 1) + cw[0:1, :] * back(u, 2)
    yc_ref[...] = (cb_ref[...].astype(F32) * acc[HALO:, :]).astype(yc_ref.dtype)

    p = with_halo(pu_h_ref, pu_ref)
    t_glob = s_idx * ts + lax.broadcasted_iota(jnp.int32, (ts, 1), 0)
    for g, w in enumerate(POOL_WINDOWS):
        pg = p[:, g * HEAD:(g + 1) * HEAD]
        win = pg
        k = 1
        while k < w:
            win = win + back(win, k)
            k *= 2
        count = jnp.minimum(t_glob + 1, w).astype(F32)
        pooled = win[HALO:, :] / count - pg[HALO:, :]
        y = jnp.dot(pooled.astype(BF16), pw_ref[g], preferred_element_type=F32)
        yp_ref[:, g * HEAD:(g + 1) * HEAD] = (y * ps_ref[:, g * HEAD:(g + 1) * HEAD]).astype(yp_ref.dtype)


def _convpool(proj, conv_w, pool_w, pool_scale, batch, seq, ts=512):
    t = proj.shape[0]
    ts = min(ts, seq)
    nt = seq // ts
    hb = ts // HALO

    def main(col):
        return pl.BlockSpec((ts, GROUP), lambda b, s: (b * nt + s, col))

    def halo(col):
        return pl.BlockSpec((HALO, GROUP), lambda b, s: (jnp.maximum((b * nt + s) * hb - 1, 0), col))

    out = pl.BlockSpec((ts, GROUP), lambda b, s: (b * nt + s, 0))
    return pl.pallas_call(
        functools.partial(_convpool_kernel, ts=ts),
        grid=(batch, nt),
        in_specs=[main(0), main(1), main(2), main(9), halo(1), halo(2), halo(9),
                  pl.BlockSpec((3, GROUP), lambda b, s: (0, 0)),
                  pl.BlockSpec((len(POOL_WINDOWS), HEAD, HEAD), lambda b, s: (0, 0, 0)),
                  pl.BlockSpec((1, GROUP), lambda b, s: (0, 0))],
        out_specs=[out, out],
        out_shape=[jax.ShapeDtypeStruct((t, GROUP), BF16)] * 2,
        compiler_params=_cparams(("parallel", "arbitrary")),
        name="convpool",
    )(proj, proj, proj, proj, proj, proj, proj, conv_w, pool_w, pool_scale)


def _sb_kernel(q_ref, k_ref, v_ref, qg_ref, kg_ref, m2_ref, o_ref, qn_ref, kn_ref, rem_ref, acc_ref, *, seq):
    def norm(a, g):
        a = a.astype(F32)
        return a * lax.rsqrt(jnp.mean(a * a, axis=-1, keepdims=True) + EPS) * g

    scale = HEAD ** -0.5
    qn_ref[...] = (norm(q_ref[...], qg_ref[...]) * scale).astype(BF16)
    kn_ref[...] = norm(k_ref[...], kg_ref[...]).astype(BF16)
    bsz = SB_BLOCK
    row = lax.broadcasted_iota(jnp.int32, (bsz, bsz), 0)
    col = lax.broadcasted_iota(jnp.int32, (bsz, bsz), 1)
    strict = col < row

    def mask_diag(a):
        top = jnp.where(strict, a[:bsz], 0.0)
        return top if a.shape[0] == bsz else jnp.concatenate([top, a[bsz:]], axis=0)

    def column(kj):
        k0 = kj * bsz
        n = seq - k0
        kb = kn_ref[pl.ds(k0, bsz), :]
        vb = v_ref[pl.ds(k0, bsz), :]
        z = lax.dot_general(qn_ref[pl.ds(k0, n), :], kb, (((1,), (1,)), ((), ())),
                            preferred_element_type=F32)
        log_rem = -_softplus(z)
        log_a = z + log_rem
        log_rem = mask_diag(log_rem)
        hi, lo = _split_bf16(log_rem)
        log_a = log_a + jnp.dot(jnp.concatenate([hi, lo], axis=1), m2_ref[...], preferred_element_type=F32)
        total = jnp.sum(log_rem, axis=1, keepdims=True)
        lower = pl.ds(k0 + bsz, n - bsz)
        if n > bsz:
            rem = rem_ref[lower, :]
            log_a = jnp.concatenate([log_a[:bsz], log_a[bsz:] + jnp.concatenate([rem] * (bsz // HEAD), axis=1)],
                                    axis=0)
        a = mask_diag(jnp.exp(log_a))
        pv = jnp.dot(a.astype(BF16), vb, preferred_element_type=F32)
        acc_ref[pl.ds(k0, bsz), :] = pv[:bsz]
        rem_ref[pl.ds(k0, bsz), :] = jnp.broadcast_to(total[:bsz], (bsz, HEAD))
        if n > bsz:
            acc_ref[lower, :] += pv[bsz:]
            rem_ref[lower, :] = rem + total[bsz:]

    for kj in reversed(range(seq // bsz)):
        column(kj)

    o_ref[...] = acc_ref[...].astype(o_ref.dtype)


def _sb_consts():
    j = np.arange(SB_BLOCK)[:, None]
    s = np.arange(SB_BLOCK)[None, :]
    m = (j > s).astype(np.float32)
    return jnp.asarray(np.concatenate([m, m], axis=0), BF16)


def _stickbreak(proj, q_g, k_g, batch, seq):
    t = proj.shape[0]
    q0, k0, v0 = 3 * N_HEADS, 4 * N_HEADS, 5 * N_HEADS

    def col(c0):
        return pl.BlockSpec((seq, HEAD), lambda b, h: (b, c0 + h))

    const = lambda shape: pl.BlockSpec(shape, lambda b, h: (0, 0))
    return pl.pallas_call(
        functools.partial(_sb_kernel, seq=seq),
        grid=(batch, N_HEADS),
        in_specs=[col(q0), col(k0), col(v0), const((1, HEAD)), const((1, HEAD)),
                  const((2 * SB_BLOCK, SB_BLOCK))],
        out_specs=pl.BlockSpec((seq, HEAD), lambda b, h: (b, h)),
        out_shape=jax.ShapeDtypeStruct((t, GROUP), BF16),
        scratch_shapes=[pltpu.VMEM((seq, HEAD), BF16), pltpu.VMEM((seq, HEAD), BF16),
                        pltpu.VMEM((seq, HEAD), F32), pltpu.VMEM((seq, HEAD), F32)],
        compiler_params=_cparams(("parallel", "parallel")),
        name="stickbreak",
    )(proj, proj, proj, q_g, k_g, _sb_consts())


def _gla_consts():
    c = CHUNK
    t = np.arange(c)[:, None]
    j = np.arange(c)[None, :]
    blocks = [(j <= t)]
    qf, kf, masks = [], [], []
    for l in range(N_LEVELS):
        n = 1 << l
        mid = (t // (2 * n)) * (2 * n) + n - 1
        odd = (t // n) % 2 == 1
        qf.append(odd & (j > mid) & (j <= t))
        kf.append((~odd) & (j > t) & (j <= mid))
        tt, ss = np.arange(c)[:, None], np.arange(c)[None, :]
        masks.append((tt // (2 * n) == ss // (2 * n)) & ((tt // n) % 2 == 1) & ((ss // n) % 2 == 0))
    masks.append(np.eye(c, dtype=bool))
    blocks += qf + kf + [(j > t)]
    a = np.concatenate(blocks, axis=0).astype(np.float32)
    a2 = np.concatenate([a, a], axis=1)
    m = np.stack(masks).astype(np.float32)
    m = np.tile(m, (1, N_HEADS, N_HEADS))
    return jnp.asarray(a2, BF16), jnp.asarray(m, F32)


def _gla_kernel(q_ref, k_ref, v_ref, r_ref, a_ref, aw_ref, ab_ref, ng_ref, a2_ref, m_ref,
                o_ref, st_ref, *, seq):
    c = CHUNK
    st_ref[...] = jnp.zeros_like(st_ref)
    lane_head = lax.broadcasted_iota(jnp.int32, (1, GLA_KW), 1) // GLA_DK
    nt = (((1,), (1,)), ((), ()))
    tn = (((0,), (0,)), ((), ()))

    def stack_heads(a):
        return jnp.concatenate([jnp.where(lane_head == h, a, 0.0) for h in range(N_HEADS)], axis=0)

    def tile_rows(a):
        return jnp.concatenate([a] * N_HEADS, axis=0)

    def chunk(ci, carry):
        r0 = pl.multiple_of(ci * c, c)
        rows = pl.ds(r0, c)
        xg = jnp.dot(a_ref[rows, :], aw_ref[...], preferred_element_type=F32) + ab_ref[...]
        g = -_softplus(-xg) * (1.0 / GLA_TAU)
        g_hi, g_lo = _split_bf16(g)
        e = jnp.exp(jnp.dot(a2_ref[...], jnp.concatenate([g_hi, g_lo], axis=0),
                            preferred_element_type=F32))
        eb = e[0:c]
        ekd = e[(1 + 2 * N_LEVELS) * c:(2 + 2 * N_LEVELS) * c]

        q = q_ref[rows, :].astype(F32) * (GLA_DK ** -0.5)
        k = k_ref[rows, :].astype(F32)
        v = v_ref[rows, :]
        v_cat = jnp.concatenate([v[:, h * HEAD:(h + 1) * HEAD] for h in range(N_HEADS)], axis=0)
        qs = stack_heads(q)
        ks = stack_heads(k)

        att = m_ref[N_LEVELS] * lax.dot_general(qs.astype(BF16), ks.astype(BF16), nt,
                                                preferred_element_type=F32)
        for l in range(N_LEVELS):
            eq = tile_rows(e[(1 + l) * c:(2 + l) * c])
            ek = tile_rows(e[(1 + N_LEVELS + l) * c:(2 + N_LEVELS + l) * c])
            p = lax.dot_general((qs * eq).astype(BF16), (ks * ek).astype(BF16), nt,
                                preferred_element_type=F32)
            att = att + m_ref[l] * p
        o = jnp.dot(att.astype(BF16), v_cat, preferred_element_type=F32)

        st = st_ref[...]
        o = o + lax.dot_general((qs * tile_rows(eb)).astype(BF16), st.astype(BF16), nt,
                                preferred_element_type=F32)
        upd = lax.dot_general(v_cat, (ks * tile_rows(ekd)).astype(BF16), tn,
                              preferred_element_type=F32)
        st_ref[...] = st * eb[c - 1:c, :] + upd

        o = o * lax.rsqrt(jnp.mean(o * o, axis=-1, keepdims=True) + EPS) * ng_ref[...]
        o = jnp.concatenate([o[h * c:(h + 1) * c] for h in range(N_HEADS)], axis=1)
        r = r_ref[rows, :].astype(F32)
        o_ref[rows, :] = (o * (r * jax.nn.sigmoid(r))).astype(o_ref.dtype)
        return carry

    lax.fori_loop(0, seq // c, chunk, 0, unroll=2)


def _gla(proj, a_w, a_b, norm_g, batch, seq):
    t = proj.shape[0]
    a2, masks = _gla_consts()

    def col(width, idx):
        return pl.BlockSpec((seq, width), lambda b: (b, idx))

    def const(shape):
        return pl.BlockSpec(shape, lambda b: (0,) * len(shape))

    return pl.pallas_call(
        functools.partial(_gla_kernel, seq=seq),
        grid=(batch,),
        in_specs=[col(GLA_KW, 12), col(GLA_KW, 13), col(GROUP, 7), col(GROUP, 8), col(HEAD, 40),
                  const(a_w.shape), const(a_b.shape), const(norm_g.shape),
                  const(a2.shape), const(masks.shape)],
        out_specs=pl.BlockSpec((seq, GROUP), lambda b: (b, 0)),
        out_shape=jax.ShapeDtypeStruct((t, GROUP), BF16),
        scratch_shapes=[pltpu.VMEM((HEAD, GLA_KW), F32)],
        compiler_params=_cparams(("parallel",)),
        name="gla",
    )(proj, proj, proj, proj, proj, a_w, a_b, norm_g, a2, masks)


def _outproj_kernel(x_ref, yc_ref, ys_ref, yg_ref, yp_ref, w_ref, o_ref):
    mixed = jnp.concatenate([yc_ref[...], ys_ref[...], yg_ref[...], yp_ref[...]], axis=1)
    o_ref[...] = x_ref[...] + jnp.dot(mixed, w_ref[...], preferred_element_type=F32)


def _outproj(x2d, ys, w, tm=512):
    t, d = x2d.shape
    tm = min(tm, t)
    row = lambda width: pl.BlockSpec((tm, width), lambda i: (i, 0))
    return pl.pallas_call(
        _outproj_kernel,
        grid=(t // tm,),
        in_specs=[row(d)] + [row(GROUP)] * 4 + [pl.BlockSpec((d, d), lambda i: (0, 0))],
        out_specs=row(d),
        out_shape=jax.ShapeDtypeStruct((t, d), F32),
        compiler_params=_cparams(("parallel",)),
        name="outproj",
    )(x2d, *ys, w)


def _ffn_kernel(x_ref, g_ref, wg_ref, wu_ref, wd_ref, o_ref, h_ref):
    @pl.when(pl.program_id(1) == 0)
    def _():
        x = x_ref[...]
        ms = jnp.mean(x * x, axis=-1, keepdims=True)
        h_ref[...] = (x * lax.rsqrt(ms + EPS) * g_ref[...]).astype(BF16)
        o_ref[...] = x

    h = h_ref[...]
    gate = jnp.dot(h, wg_ref[...], preferred_element_type=F32)
    up = jnp.dot(h, wu_ref[...], preferred_element_type=F32)
    act = (gate * jax.nn.sigmoid(gate) * up).astype(BF16)
    o_ref[...] += jnp.dot(act, wd_ref[...], preferred_element_type=F32)


def _ffn(x2d, g, wg, wu, wd, tm=1024, tf=512):
    t, d = x2d.shape
    f = wg.shape[1]
    tm = min(tm, t)
    return pl.pallas_call(
        _ffn_kernel,
        grid=(t // tm, f // tf),
        in_specs=[
            pl.BlockSpec((tm, d), lambda i, j: (i, 0)),
            pl.BlockSpec((1, d), lambda i, j: (0, 0)),
            pl.BlockSpec((d, tf), lambda i, j: (0, j)),
            pl.BlockSpec((d, tf), lambda i, j: (0, j)),
            pl.BlockSpec((tf, d), lambda i, j: (j, 0)),
        ],
        out_specs=pl.BlockSpec((tm, d), lambda i, j: (i, 0)),
        out_shape=jax.ShapeDtypeStruct((t, d), F32),
        scratch_shapes=[pltpu.VMEM((tm, d), BF16)],
        compiler_params=_cparams(("parallel", "arbitrary")),
        name="ffn",
    )(x2d, g, wg, wu, wd)


def _prep_w_in(w):
    d = w.shape[0]
    ga0 = IN_COLS - GROUP - GLA_RANK
    pad = jnp.zeros((d, PROJ_COLS - IN_COLS), w.dtype)
    return jnp.concatenate([w[:, :ga0], w[:, ga0 + GLA_RANK:], w[:, ga0:ga0 + GLA_RANK], pad], axis=1).astype(BF16)


def kernel(x, norm1_g, w_in, conv_w, sb_q_g, sb_k_g, gla_a_w, gla_a_b, gla_norm_g, pool_w, pool_scale,
           w_out, norm2_g, w_gate, w_up, w_down):
    batch, seq, d = x.shape
    depth = w_in.shape[0]
    x2d = x.reshape(batch * seq, d)
    for l in range(depth):
        proj = _inproj(x2d, norm1_g[l][None, :], _prep_w_in(w_in[l]))
        y_conv, y_pool = _convpool(proj, conv_w[l], pool_w[l].astype(BF16), pool_scale[l][None, :], batch, seq)
        y_sb = _stickbreak(proj, sb_q_g[l][None, :], sb_k_g[l][None, :], batch, seq)
        a_w = jnp.pad(gla_a_w[l], ((0, HEAD - GLA_RANK), (0, 0))).astype(BF16)
        y_gla = _gla(proj, a_w, gla_a_b[l][None, :], gla_norm_g[l][None, :], batch, seq)
        x2d = _outproj(x2d, (y_conv, y_sb, y_gla, y_pool), w_out[l].astype(BF16))
        x2d = _ffn(x2d, norm2_g[l][None, :], w_gate[l].astype(BF16), w_up[l].astype(BF16), w_down[l].astype(BF16))
    return x2d.reshape(batch, seq, d)
```

```python
import functools

import numpy as np
import jax
import jax.numpy as jnp
from jax import lax
from jax.experimental import pallas as pl
from jax.experimental.pallas import tpu as pltpu

F32 = jnp.float32
BF16 = jnp.bfloat16

D_MODEL = 2048
GROUP = D_MODEL // 4
HEAD = 128
N_HEADS = GROUP // HEAD
GLA_DK = 64
GLA_KW = N_HEADS * GLA_DK
GLA_RANK = 16
GLA_TAU = 16.0
CHUNK = 64
N_LEVELS = 6
GLA_GROUP = 8
POOL_WINDOWS = (2, 4, 8, 16)
HALO = 16
D_FF = 5632
EPS = 1e-6
IN_COLS = 5136
PROJ_COLS = 5376
SB_BLOCK = 256

VMEM_LIMIT = 56 * 1024 * 1024


def _cparams(sem):
    return pltpu.CompilerParams(dimension_semantics=sem, vmem_limit_bytes=VMEM_LIMIT)


def _softplus(z):
    return jnp.maximum(z, 0.0) + jnp.log(1.0 + jnp.exp(-jnp.abs(z)))


def _split_bf16(a):
    hi = a.astype(BF16)
    lo = (a - hi.astype(F32)).astype(BF16)
    return hi, lo


def _inproj_kernel(x_ref, g_ref, w_ref, o_ref, h_ref):
    @pl.when(pl.program_id(1) == 0)
    def _():
        x = x_ref[...]
        ms = jnp.mean(x * x, axis=-1, keepdims=True)
        h_ref[...] = (x * lax.rsqrt(ms + EPS) * g_ref[...]).astype(BF16)

    o_ref[...] = jnp.dot(h_ref[...], w_ref[...], preferred_element_type=F32).astype(o_ref.dtype)


def _inproj(x2d, g, w, layer, tm=1024, tn=1792):
    t, d = x2d.shape
    n = w.shape[2]
    tm = min(tm, t)
    return pl.pallas_call(
        _inproj_kernel,
        grid=(t // tm, n // tn),
        in_specs=[
            pl.BlockSpec((tm, d), lambda i, j: (i, 0)),
            pl.BlockSpec((1, d), lambda i, j: (0, 0)),
            pl.BlockSpec((None, d, tn), lambda i, j: (layer, 0, j)),
        ],
        out_specs=pl.BlockSpec((tm, tn), lambda i, j: (i, j)),
        out_shape=jax.ShapeDtypeStruct((t, n), BF16),
        scratch_shapes=[pltpu.VMEM((tm, d), BF16)],
        compiler_params=_cparams(("parallel", "arbitrary")),
        name="inproj",
    )(x2d, g, w)


def _convpool_values(cb_ref, cc_ref, ch_ref, pu_ref, cc_h_ref, ch_h_ref, pu_h_ref, cw_ref, pw_ref, ps_ref,
                     tile_in_seq, ts):
    keep = (tile_in_seq > 0).astype(F32)

    def with_halo(h_ref, m_ref):
        return jnp.concatenate([h_ref[...].astype(F32) * keep, m_ref[...].astype(F32)], axis=0)

    def back(a, k):
        return pltpu.roll(a, k, axis=0)

    u = with_halo(cc_h_ref, cc_ref) * with_halo(ch_h_ref, ch_ref)
    cw = cw_ref[...]
    acc = cw[2:3, :] * u + cw[1:2, :] * back(u, 1) + cw[0:1, :] * back(u, 2)
    y_conv = (cb_ref[...].astype(F32) * acc[HALO:, :]).astype(BF16)

    p = with_halo(pu_h_ref, pu_ref)
    t_seq = tile_in_seq * ts + lax.broadcasted_iota(jnp.int32, (ts, 1), 0)
    y_pool = []
    for g, w in enumerate(POOL_WINDOWS):
        pg = p[:, g * HEAD:(g + 1) * HEAD]
        win = pg
        k = 1
        while k < w:
            win = win + back(win, k)
            k *= 2
        count = jnp.minimum(t_seq + 1, w).astype(F32)
        pooled = win[HALO:, :] / count - pg[HALO:, :]
        y = jnp.dot(pooled.astype(BF16), pw_ref[g], preferred_element_type=F32)
        y_pool.append((y * ps_ref[:, g * HEAD:(g + 1) * HEAD]).astype(BF16))
    return y_conv, jnp.concatenate(y_pool, axis=1)


def _sb_kernel(q_ref, k_ref, v_ref, qg_ref, kg_ref, m2_ref, o_ref, qn_ref, kn_ref, rem_ref, acc_ref, *, seq):
    def norm(a, g):
        a = a.astype(F32)
        return a * lax.rsqrt(jnp.mean(a * a, axis=-1, keepdims=True) + EPS) * g

    scale = HEAD ** -0.5
    qn_ref[...] = (norm(q_ref[...], qg_ref[...]) * scale).astype(BF16)
    kn_ref[...] = norm(k_ref[...], kg_ref[...]).astype(BF16)
    bsz = SB_BLOCK
    row = lax.broadcasted_iota(jnp.int32, (bsz, bsz), 0)
    col = lax.broadcasted_iota(jnp.int32, (bsz, bsz), 1)
    strict = col < row

    def mask_diag(a):
        top = jnp.where(strict, a[:bsz], 0.0)
        return top if a.shape[0] == bsz else jnp.concatenate([top, a[bsz:]], axis=0)

    def column(kj):
        k0 = kj * bsz
        n = seq - k0
        kb = kn_ref[pl.ds(k0, bsz), :]
        vb = v_ref[pl.ds(k0, bsz), :]
        z = lax.dot_general(qn_ref[pl.ds(k0, n), :], kb, (((1,), (1,)), ((), ())),
                            preferred_element_type=F32)
        log_rem = -_softplus(z)
        log_a = z + log_rem
        log_rem = mask_diag(log_rem)
        hi, lo = _split_bf16(log_rem)
        log_a = log_a + jnp.dot(jnp.concatenate([hi, lo], axis=1), m2_ref[...], preferred_element_type=F32)
        total = jnp.sum(log_rem, axis=1, keepdims=True)
        lower = pl.ds(k0 + bsz, n - bsz)
        if n > bsz:
            rem = rem_ref[lower, :]
            log_a = jnp.concatenate([log_a[:bsz], log_a[bsz:] + jnp.concatenate([rem] * (bsz // HEAD), axis=1)],
                                    axis=0)
        a = mask_diag(jnp.exp(log_a))
        pv = jnp.dot(a.astype(BF16), vb, preferred_element_type=F32)
        acc_ref[pl.ds(k0, bsz), :] = pv[:bsz]
        rem_ref[pl.ds(k0, bsz), :] = jnp.broadcast_to(total[:bsz], (bsz, HEAD))
        if n > bsz:
            acc_ref[lower, :] += pv[bsz:]
            rem_ref[lower, :] = rem + total[bsz:]

    for kj in reversed(range(seq // bsz)):
        column(kj)

    o_ref[...] = acc_ref[...].astype(o_ref.dtype)


def _sb_consts():
    j = np.arange(SB_BLOCK)[:, None]
    s = np.arange(SB_BLOCK)[None, :]
    m = (j > s).astype(np.float32)
    return jnp.asarray(np.concatenate([m, m], axis=0), BF16)


def _stickbreak(proj, q_g, k_g, batch, seq):
    t = proj.shape[0]
    q0, k0, v0 = 3 * N_HEADS, 4 * N_HEADS, 5 * N_HEADS

    def col(c0):
        return pl.BlockSpec((seq, HEAD), lambda b, h: (b, c0 + h))

    const = lambda shape: pl.BlockSpec(shape, lambda b, h: (0, 0))
    return pl.pallas_call(
        functools.partial(_sb_kernel, seq=seq),
        grid=(batch, N_HEADS),
        in_specs=[col(q0), col(k0), col(v0), const((1, HEAD)), const((1, HEAD)),
                  const((2 * SB_BLOCK, SB_BLOCK))],
        out_specs=pl.BlockSpec((seq, HEAD), lambda b, h: (b, h)),
        out_shape=jax.ShapeDtypeStruct((t, GROUP), BF16),
        scratch_shapes=[pltpu.VMEM((seq, HEAD), BF16), pltpu.VMEM((seq, HEAD), BF16),
                        pltpu.VMEM((seq, HEAD), F32), pltpu.VMEM((seq, HEAD), F32)],
        compiler_params=_cparams(("parallel", "parallel")),
        name="stickbreak",
    )(proj, proj, proj, q_g, k_g, _sb_consts())


def _gla_consts():
    c = CHUNK
    t = np.arange(c)[:, None]
    j = np.arange(c)[None, :]
    blocks = [(j <= t)]
    qf, kf, masks = [], [], []
    for l in range(N_LEVELS):
        n = 1 << l
        mid = (t // (2 * n)) * (2 * n) + n - 1
        odd = (t // n) % 2 == 1
        qf.append(odd & (j > mid) & (j <= t))
        kf.append((~odd) & (j > t) & (j <= mid))
        tt, ss = np.arange(c)[:, None], np.arange(c)[None, :]
        masks.append((tt // (2 * n) == ss // (2 * n)) & ((tt // n) % 2 == 1) & ((ss // n) % 2 == 0))
    masks.append(np.eye(c, dtype=bool))
    blocks += qf + kf + [(j > t)]
    a = np.concatenate(blocks, axis=0).astype(np.float32)
    a2 = np.concatenate([a, a], axis=1)
    m = np.stack(masks).astype(np.float32)
    m = np.tile(m, (1, 1, N_HEADS))
    return jnp.asarray(a2, BF16), jnp.asarray(m, F32)


def _gla_kernel(q_ref, k_ref, v_ref, r_ref, a_ref, aw_ref, ab_ref, ng_ref, a2_ref, m_ref,
                o_ref, st_ref, *, seq, n_grp):
    c = CHUNK
    st_ref[...] = jnp.zeros_like(st_ref)
    lane_head = lax.broadcasted_iota(jnp.int32, (1, GLA_KW), 1) // GLA_DK
    nt = (((1,), (1,)), ((), ()))
    tn = (((0,), (0,)), ((), ()))

    def stack_heads(a):
        zero = jnp.zeros_like(a)
        return jnp.concatenate([jnp.where(lane_head == h, a, zero) for h in range(N_HEADS)], axis=0)

    def block_diag_values(v):
        zero = jnp.zeros((c, HEAD), v.dtype)
        return jnp.concatenate(
            [jnp.concatenate([v[:, h * HEAD:(h + 1) * HEAD] if hh == h else zero for hh in range(N_HEADS)], axis=1)
             for h in range(N_HEADS)], axis=0)

    def e_block(e, i):
        return e[i * c:(i + 1) * c]

    def group(gi, carry):
        r0 = pl.multiple_of(gi * (n_grp * c), n_grp * c)
        rows = pl.ds(r0, n_grp * c)
        cs = range(n_grp)
        xg = jnp.dot(a_ref[rows, :], aw_ref[...], preferred_element_type=F32) + ab_ref[...]
        g = -_softplus(-xg) * (1.0 / GLA_TAU)
        g_hi, g_lo = _split_bf16(g)
        es = [jnp.dot(a2_ref[...], jnp.concatenate([e_block(g_hi, i), e_block(g_lo, i)], axis=0),
                      preferred_element_type=F32) for i in cs]
        es = [jnp.exp(e) for e in es]

        q = q_ref[rows, :].astype(F32) * (GLA_DK ** -0.5)
        k = k_ref[rows, :].astype(F32)
        v = v_ref[rows, :]
        qc = [e_block(q, i) for i in cs]
        kc = [e_block(k, i) for i in cs]
        vc = [e_block(v, i) for i in cs]

        att = [m_ref[N_LEVELS] * lax.dot_general(qc[i].astype(BF16), stack_heads(kc[i].astype(BF16)), nt,
                                                 preferred_element_type=F32) for i in cs]
        for l in range(N_LEVELS):
            ql = [(qc[i] * e_block(es[i], 1 + l)).astype(BF16) for i in cs]
            kl = [stack_heads((kc[i] * e_block(es[i], 1 + N_LEVELS + l)).astype(BF16)) for i in cs]
            p = [lax.dot_general(ql[i], kl[i], nt, preferred_element_type=F32) for i in cs]
            att = [att[i] + m_ref[l] * p[i] for i in cs]
        o = [jnp.dot(att[i].astype(BF16), block_diag_values(vc[i]), preferred_element_type=F32) for i in cs]

        qd = [(qc[i] * e_block(es[i], 0)).astype(BF16) for i in cs]
        kd = [stack_heads((kc[i] * e_block(es[i], 1 + 2 * N_LEVELS)).astype(BF16)) for i in cs]
        v_cat = [jnp.concatenate([vc[i][:, h * HEAD:(h + 1) * HEAD] for h in range(N_HEADS)], axis=0) for i in cs]
        upd = [lax.dot_general(v_cat[i], kd[i], tn, preferred_element_type=F32) for i in cs]
        st = [st_ref[...]]
        for i in cs:
            st.append(st[i] * es[i][c - 1:c, :] + upd[i])
        st_ref[...] = st[n_grp]
        o = [o[i] + lax.dot_general(qd[i], stack_heads(st[i].astype(BF16)), nt, preferred_element_type=F32)
             for i in cs]

        o = jnp.concatenate(o, axis=0)
        oh = [o[:, h * HEAD:(h + 1) * HEAD] for h in range(N_HEADS)]
        oh = [a * lax.rsqrt(jnp.mean(a * a, axis=-1, keepdims=True) + EPS) * ng_ref[...] for a in oh]
        r = r_ref[rows, :].astype(F32)
        o_ref[rows, :] = (jnp.concatenate(oh, axis=1) * (r * jax.nn.sigmoid(r))).astype(o_ref.dtype)
        return carry

    lax.fori_loop(0, seq // (n_grp * c), group, 0)


def _gla(proj, a_w, a_b, norm_g, batch, seq):
    t = proj.shape[0]
    a2, masks = _gla_consts()

    def col(width, idx):
        return pl.BlockSpec((seq, width), lambda b: (b, idx))

    def const(shape):
        return pl.BlockSpec(shape, lambda b: (0,) * len(shape))

    return pl.pallas_call(
        functools.partial(_gla_kernel, seq=seq, n_grp=GLA_GROUP),
        grid=(batch,),
        in_specs=[col(GLA_KW, 12), col(GLA_KW, 13), col(GROUP, 7), col(GROUP, 8), col(HEAD, 40),
                  const(a_w.shape), const(a_b.shape), const(norm_g.shape),
                  const(a2.shape), const(masks.shape)],
        out_specs=pl.BlockSpec((seq, GROUP), lambda b: (b, 0)),
        out_shape=jax.ShapeDtypeStruct((t, GROUP), BF16),
        scratch_shapes=[pltpu.VMEM((HEAD, GLA_KW), F32)],
        compiler_params=_cparams(("parallel",)),
        name="gla",
    )(proj, proj, proj, proj, proj, a_w, a_b, norm_g, a2, masks)


def _outproj_kernel(x_ref, cb_ref, cc_ref, ch_ref, pu_ref, cc_h_ref, ch_h_ref, pu_h_ref, cw_ref, pw_ref, ps_ref,
                    ys_ref, yg_ref, w_ref, o_ref, *, tm, tiles_per_seq):
    tile_in_seq = pl.program_id(0) % tiles_per_seq
    y_conv, y_pool = _convpool_values(cb_ref, cc_ref, ch_ref, pu_ref, cc_h_ref, ch_h_ref, pu_h_ref,
                                      cw_ref, pw_ref, ps_ref, tile_in_seq, tm)
    mixed = jnp.concatenate([y_conv, ys_ref[...], yg_ref[...], y_pool], axis=1)
    o_ref[...] = x_ref[...] + jnp.dot(mixed, w_ref[...], preferred_element_type=F32)


def _outproj(x2d, proj, y_sb, y_gla, conv_w, pool_w, pool_scale, w_out, layer, seq, tm=512):
    t, d = x2d.shape
    tm = min(tm, seq)
    hb = tm // HALO
    row = lambda width: pl.BlockSpec((tm, width), lambda i: (i, 0))
    main = lambda col: pl.BlockSpec((tm, GROUP), lambda i: (i, col))
    halo = lambda col: pl.BlockSpec((HALO, GROUP), lambda i: (jnp.maximum(i * hb - 1, 0), col))
    const = lambda shape: pl.BlockSpec(shape, lambda i: (0,) * len(shape))
    return pl.pallas_call(
        functools.partial(_outproj_kernel, tm=tm, tiles_per_seq=seq // tm),
        grid=(t // tm,),
        in_specs=[row(d), main(0), main(1), main(2), main(9), halo(1), halo(2), halo(9),
                  const(conv_w.shape), const(pool_w.shape), const(pool_scale.shape),
                  row(GROUP), row(GROUP),
                  pl.BlockSpec((None, d, d), lambda i: (layer, 0, 0))],
        out_specs=row(d),
        out_shape=jax.ShapeDtypeStruct((t, d), F32),
        compiler_params=_cparams(("parallel",)),
        name="outproj",
    )(x2d, proj, proj, proj, proj, proj, proj, proj, conv_w, pool_w, pool_scale, y_sb, y_gla, w_out)


def _ffn_kernel(x_ref, g_ref, wg_ref, wu_ref, wd_ref, o_ref, h_ref):
    @pl.when(pl.program_id(1) == 0)
    def _():
        x = x_ref[...]
        ms = jnp.mean(x * x, axis=-1, keepdims=True)
        h_ref[...] = (x * lax.rsqrt(ms + EPS) * g_ref[...]).astype(BF16)
        o_ref[...] = x

    h = h_ref[...]
    gate = jnp.dot(h, wg_ref[...], preferred_element_type=F32)
    up = jnp.dot(h, wu_ref[...], preferred_element_type=F32)
    act = (gate * jax.nn.sigmoid(gate) * up).astype(BF16)
    o_ref[...] += jnp.dot(act, wd_ref[...], preferred_element_type=F32)


def _ffn(x2d, g, wg, wu, wd, layer, tm=1024, tf=512):
    t, d = x2d.shape
    f = wg.shape[2]
    tm = min(tm, t)
    return pl.pallas_call(
        _ffn_kernel,
        grid=(t // tm, f // tf),
        in_specs=[
            pl.BlockSpec((tm, d), lambda i, j: (i, 0)),
            pl.BlockSpec((1, d), lambda i, j: (0, 0)),
            pl.BlockSpec((None, d, tf), lambda i, j: (layer, 0, j)),
            pl.BlockSpec((None, d, tf), lambda i, j: (layer, 0, j)),
            pl.BlockSpec((None, tf, d), lambda i, j: (layer, j, 0)),
        ],
        out_specs=pl.BlockSpec((tm, d), lambda i, j: (i, 0)),
        out_shape=jax.ShapeDtypeStruct((t, d), F32),
        scratch_shapes=[pltpu.VMEM((tm, d), BF16)],
        compiler_params=_cparams(("parallel", "arbitrary")),
        name="ffn",
    )(x2d, g, wg, wu, wd)


def _prep_w_in(w):
    w = w.astype(BF16)
    ga0 = IN_COLS - GROUP - GLA_RANK
    pad = jnp.zeros(w.shape[:-1] + (PROJ_COLS - IN_COLS,), BF16)
    return jnp.concatenate([w[..., :ga0], w[..., ga0 + GLA_RANK:], w[..., ga0:ga0 + GLA_RANK], pad], axis=-1)


def kernel(x, norm1_g, w_in, conv_w, sb_q_g, sb_k_g, gla_a_w, gla_a_b, gla_norm_g, pool_w, pool_scale,
           w_out, norm2_g, w_gate, w_up, w_down):
    batch, seq, d = x.shape
    depth = w_in.shape[0]
    x2d = x.reshape(batch * seq, d)
    w_in, w_out, w_gate, w_up, w_down = _prep_w_in(w_in), w_out.astype(BF16), w_gate.astype(BF16), \
        w_up.astype(BF16), w_down.astype(BF16)
    pool_w = pool_w.astype(BF16)
    a_w = jnp.pad(gla_a_w, ((0, 0), (0, HEAD - GLA_RANK), (0, 0))).astype(BF16)
    for l in range(depth):
        proj = _inproj(x2d, norm1_g[l][None, :], w_in, l)
        y_sb = _stickbreak(proj, sb_q_g[l][None, :], sb_k_g[l][None, :], batch, seq)
        y_gla = _gla(proj, a_w[l], gla_a_b[l][None, :], gla_norm_g[l][None, :], batch, seq)
        x2d = _outproj(x2d, proj, y_sb, y_gla, conv_w[l], pool_w[l], pool_scale[l][None, :], w_out, l, seq)
        x2d = _ffn(x2d, norm2_g[l][None, :], w_gate, w_up, w_down, l)
    return x2d.reshape(batch, seq, d)
```

```python
import functools

import numpy as np
import jax
import jax.numpy as jnp
from jax import lax
from jax.experimental import pallas as pl
from jax.experimental.pallas import tpu as pltpu

F32 = jnp.float32
BF16 = jnp.bfloat16

D_MODEL = 2048
GROUP = D_MODEL // 4
HEAD = 128
N_HEADS = GROUP // HEAD
GLA_DK = 64
GLA_KW = N_HEADS * GLA_DK
GLA_RANK = 16
GLA_TAU = 16.0
CHUNK = 64
N_LEVELS = 6
GLA_GROUP = 8
POOL_WINDOWS = (2, 4, 8, 16)
HALO = 16
D_FF = 5632
EPS = 1e-6
IN_COLS = 5136
PROJ_COLS = 5376
SB_BLOCK = 256

VMEM_LIMIT = 56 * 1024 * 1024


def _cparams(sem):
    return pltpu.CompilerParams(dimension_semantics=sem, vmem_limit_bytes=VMEM_LIMIT)


def _softplus(z):
    neg_abs = lax.bitcast_convert_type(lax.bitcast_convert_type(z, jnp.uint32) | jnp.uint32(0x80000000), F32)
    return jnp.maximum(z, 0.0) + jnp.log(1.0 + jnp.exp(neg_abs))


def _split_bf16(a):
    hi = a.astype(BF16)
    lo = (a - hi.astype(F32)).astype(BF16)
    return hi, lo


def _inproj_kernel(x_ref, g_ref, w_ref, o_ref, h_ref):
    @pl.when(pl.program_id(1) == 0)
    def _():
        x = x_ref[...]
        ms = jnp.mean(x * x, axis=-1, keepdims=True)
        h_ref[...] = (x * lax.rsqrt(ms + EPS) * g_ref[...]).astype(BF16)

    o_ref[...] = jnp.dot(h_ref[...], w_ref[...], preferred_element_type=F32).astype(o_ref.dtype)


def _inproj(x2d, g, w, layer, tm=1024, tn=1792):
    t, d = x2d.shape
    n = w.shape[2]
    tm = min(tm, t)
    return pl.pallas_call(
        _inproj_kernel,
        grid=(t // tm, n // tn),
        in_specs=[
            pl.BlockSpec((tm, d), lambda i, j: (i, 0)),
            pl.BlockSpec((1, d), lambda i, j: (0, 0)),
            pl.BlockSpec((None, d, tn), lambda i, j: (layer, 0, j)),
        ],
        out_specs=pl.BlockSpec((tm, tn), lambda i, j: (i, j)),
        out_shape=jax.ShapeDtypeStruct((t, n), BF16),
        scratch_shapes=[pltpu.VMEM((tm, d), BF16)],
        compiler_params=_cparams(("parallel", "arbitrary")),
        name="inproj",
    )(x2d, g, w)


def _convpool_values(cb_ref, cc_ref, ch_ref, pu_ref, cc_h_ref, ch_h_ref, pu_h_ref, cw_ref, pw_ref, ps_ref,
                     tile_in_seq, ts):
    keep = (tile_in_seq > 0).astype(F32)

    def with_halo(h_ref, m_ref):
        return jnp.concatenate([h_ref[...].astype(F32) * keep, m_ref[...].astype(F32)], axis=0)

    def back(a, k):
        return pltpu.roll(a, k, axis=0)

    u = with_halo(cc_h_ref, cc_ref) * with_halo(ch_h_ref, ch_ref)
    cw = cw_ref[...]
    acc = cw[2:3, :] * u + cw[1:2, :] * back(u, 1) + cw[0:1, :] * back(u, 2)
    y_conv = (cb_ref[...].astype(F32) * acc[HALO:, :]).astype(BF16)

    p = with_halo(pu_h_ref, pu_ref)
    t_seq = tile_in_seq * ts + lax.broadcasted_iota(jnp.int32, (ts, 1), 0)
    y_pool = []
    for g, w in enumerate(POOL_WINDOWS):
        pg = p[:, g * HEAD:(g + 1) * HEAD]
        win = pg
        k = 1
        while k < w:
            win = win + back(win, k)
            k *= 2
        count = jnp.minimum(t_seq + 1, w).astype(F32)
        pooled = win[HALO:, :] / count - pg[HALO:, :]
        y = jnp.dot(pooled.astype(BF16), pw_ref[g], preferred_element_type=F32)
        y_pool.append((y * ps_ref[:, g * HEAD:(g + 1) * HEAD]).astype(BF16))
    return y_conv, jnp.concatenate(y_pool, axis=1)


def _sb_kernel(q_ref, k_ref, v_ref, qg_ref, kg_ref, m_ref, o_ref, qn_ref, kn_ref, rem_ref, acc_ref, *, seq):
    def norm(a, g):
        a = a.astype(F32)
        return a * lax.rsqrt(jnp.mean(a * a, axis=-1, keepdims=True) + EPS) * g

    scale = HEAD ** -0.5
    qn_ref[...] = (norm(q_ref[...], qg_ref[...]) * scale).astype(BF16)
    kn_ref[...] = norm(k_ref[...], kg_ref[...]).astype(BF16)
    bsz = SB_BLOCK
    row = lax.broadcasted_iota(jnp.int32, (bsz, bsz), 0)
    col = lax.broadcasted_iota(jnp.int32, (bsz, bsz), 1)
    strict = col < row

    def mask_diag(a):
        top = jnp.where(strict, a[:bsz], 0.0)
        return top if a.shape[0] == bsz else jnp.concatenate([top, a[bsz:]], axis=0)

    def column(kj):
        k0 = kj * bsz
        n = seq - k0
        kb = kn_ref[pl.ds(k0, bsz), :]
        vb = v_ref[pl.ds(k0, bsz), :]
        z = lax.dot_general(qn_ref[pl.ds(k0, n), :], kb, (((1,), (1,)), ((), ())),
                            preferred_element_type=F32)
        nlr = _softplus(z)
        log_a = z - nlr
        nlr = mask_diag(nlr)
        log_a = log_a - jnp.dot(nlr.astype(BF16), m_ref[...], preferred_element_type=F32)
        total = jnp.sum(nlr, axis=1, keepdims=True)
        lower = pl.ds(k0 + bsz, n - bsz)
        if n > bsz:
            rem = rem_ref[lower, :]
            log_a = jnp.concatenate([log_a[:bsz], log_a[bsz:] - jnp.concatenate([rem] * (bsz // HEAD), axis=1)],
                                    axis=0)
        a = mask_diag(jnp.exp(log_a))
        pv = jnp.dot(a.astype(BF16), vb, preferred_element_type=F32)
        acc_ref[pl.ds(k0, bsz), :] = pv[:bsz]
        rem_ref[pl.ds(k0, bsz), :] = jnp.broadcast_to(total[:bsz], (bsz, HEAD))
        if n > bsz:
            acc_ref[lower, :] += pv[bsz:]
            rem_ref[lower, :] = rem + total[bsz:]

    for kj in reversed(range(seq // bsz)):
        column(kj)

    o_ref[...] = acc_ref[...].astype(o_ref.dtype)


def _sb_consts():
    j = np.arange(SB_BLOCK)[:, None]
    s = np.arange(SB_BLOCK)[None, :]
    return jnp.asarray((j > s).astype(np.float32), BF16)


def _stickbreak(proj, q_g, k_g, batch, seq):
    t = proj.shape[0]
    assert seq % SB_BLOCK == 0, seq
    q0, k0, v0 = 3 * N_HEADS, 4 * N_HEADS, 5 * N_HEADS

    def col(c0):
        return pl.BlockSpec((seq, HEAD), lambda b, h: (b, c0 + h))

    const = lambda shape: pl.BlockSpec(shape, lambda b, h: (0, 0))
    return pl.pallas_call(
        functools.partial(_sb_kernel, seq=seq),
        grid=(batch, N_HEADS),
        in_specs=[col(q0), col(k0), col(v0), const((1, HEAD)), const((1, HEAD)),
                  const((SB_BLOCK, SB_BLOCK))],
        out_specs=pl.BlockSpec((seq, HEAD), lambda b, h: (b, h)),
        out_shape=jax.ShapeDtypeStruct((t, GROUP), BF16),
        scratch_shapes=[pltpu.VMEM((seq, HEAD), BF16), pltpu.VMEM((seq, HEAD), BF16),
                        pltpu.VMEM((seq, HEAD), F32), pltpu.VMEM((seq, HEAD), F32)],
        compiler_params=_cparams(("parallel", "parallel")),
        name="stickbreak",
    )(proj, proj, proj, q_g, k_g, _sb_consts())


def _gla_consts():
    c = CHUNK
    t = np.arange(c)[:, None]
    j = np.arange(c)[None, :]
    blocks = [(j <= t)]
    qf, kf, masks = [], [], []
    for l in range(N_LEVELS):
        n = 1 << l
        mid = (t // (2 * n)) * (2 * n) + n - 1
        odd = (t // n) % 2 == 1
        qf.append(odd & (j > mid) & (j <= t))
        kf.append((~odd) & (j > t) & (j <= mid))
        tt, ss = np.arange(c)[:, None], np.arange(c)[None, :]
        masks.append((tt // (2 * n) == ss // (2 * n)) & ((tt // n) % 2 == 1) & ((ss // n) % 2 == 0))
    masks.append(np.eye(c, dtype=bool))
    blocks += qf + kf + [(j > t)]
    a = np.concatenate(blocks, axis=0).astype(np.float32)
    a2 = np.concatenate([a, a], axis=1)
    m = np.stack(masks).astype(np.float32)
    m = np.tile(m, (1, 1, N_HEADS))
    return jnp.asarray(a2, BF16), jnp.asarray(m, F32)


def _gla_kernel(q_ref, k_ref, v_ref, r_ref, a_ref, aw_ref, ab_ref, ng_ref, a2_ref, m_ref,
                o_ref, st_ref, *, seq, n_grp):
    c = CHUNK
    st_ref[...] = jnp.zeros_like(st_ref)
    lane_head = lax.broadcasted_iota(jnp.int32, (1, GLA_KW), 1) // GLA_DK
    nt = (((1,), (1,)), ((), ()))
    tn = (((0,), (0,)), ((), ()))

    def stack_heads(a):
        zero = jnp.zeros_like(a)
        return jnp.concatenate([jnp.where(lane_head == h, a, zero) for h in range(N_HEADS)], axis=0)

    def block_diag_values(v):
        zero = jnp.zeros((c, HEAD), v.dtype)
        return jnp.concatenate(
            [jnp.concatenate([v[:, h * HEAD:(h + 1) * HEAD] if hh == h else zero for hh in range(N_HEADS)], axis=1)
             for h in range(N_HEADS)], axis=0)

    def e_block(e, i):
        return e[i * c:(i + 1) * c]

    def group(gi, carry):
        r0 = pl.multiple_of(gi * (n_grp * c), n_grp * c)
        rows = pl.ds(r0, n_grp * c)
        cs = range(n_grp)
        xg = jnp.dot(a_ref[rows, :], aw_ref[...], preferred_element_type=F32) + ab_ref[...]
        g = -_softplus(-xg) * (1.0 / GLA_TAU)
        g_hi, g_lo = _split_bf16(g)
        es = [jnp.dot(a2_ref[...], jnp.concatenate([e_block(g_hi, i), e_block(g_lo, i)], axis=0),
                      preferred_element_type=F32) for i in cs]
        es = [jnp.exp(e) for e in es]

        q = q_ref[rows, :].astype(F32) * (GLA_DK ** -0.5)
        k = k_ref[rows, :].astype(F32)
        v = v_ref[rows, :]
        qc = [e_block(q, i) for i in cs]
        kc = [e_block(k, i) for i in cs]
        vc = [e_block(v, i) for i in cs]

        att = [m_ref[N_LEVELS] * lax.dot_general(qc[i].astype(BF16), stack_heads(kc[i].astype(BF16)), nt,
                                                 preferred_element_type=F32) for i in cs]
        for l in range(N_LEVELS):
            ql = [(qc[i] * e_block(es[i], 1 + l)).astype(BF16) for i in cs]
            kl = [stack_heads((kc[i] * e_block(es[i], 1 + N_LEVELS + l)).astype(BF16)) for i in cs]
            p = [lax.dot_general(ql[i], kl[i], nt, preferred_element_type=F32) for i in cs]
            att = [att[i] + m_ref[l] * p[i] for i in cs]
        o = [jnp.dot(att[i].astype(BF16), block_diag_values(vc[i]), preferred_element_type=F32) for i in cs]

        qd = [(qc[i] * e_block(es[i], 0)).astype(BF16) for i in cs]
        kd = [stack_heads((kc[i] * e_block(es[i], 1 + 2 * N_LEVELS)).astype(BF16)) for i in cs]
        v_cat = [jnp.concatenate([vc[i][:, h * HEAD:(h + 1) * HEAD] for h in range(N_HEADS)], axis=0) for i in cs]
        upd = [lax.dot_general(v_cat[i], kd[i], tn, preferred_element_type=F32) for i in cs]
        st = [st_ref[...]]
        for i in cs:
            st.append(st[i] * es[i][c - 1:c, :] + upd[i])
        st_ref[...] = st[n_grp]
        o = [o[i] + lax.dot_general(qd[i], stack_heads(st[i].astype(BF16)), nt, preferred_element_type=F32)
             for i in cs]

        o = jnp.concatenate(o, axis=0)
        oh = [o[:, h * HEAD:(h + 1) * HEAD] for h in range(N_HEADS)]
        oh = [a * lax.rsqrt(jnp.mean(a * a, axis=-1, keepdims=True) + EPS) * ng_ref[...] for a in oh]
        r = r_ref[rows, :].astype(F32)
        o_ref[rows, :] = (jnp.concatenate(oh, axis=1) * (r * jax.nn.sigmoid(r))).astype(o_ref.dtype)
        return carry

    lax.fori_loop(0, seq // (n_grp * c), group, 0)


def _gla(proj, a_w, a_b, norm_g, batch, seq):
    t = proj.shape[0]
    assert seq % (GLA_GROUP * CHUNK) == 0, seq
    a2, masks = _gla_consts()

    def col(width, idx):
        return pl.BlockSpec((seq, width), lambda b: (b, idx))

    def const(shape):
        return pl.BlockSpec(shape, lambda b: (0,) * len(shape))

    return pl.pallas_call(
        functools.partial(_gla_kernel, seq=seq, n_grp=GLA_GROUP),
        grid=(batch,),
        in_specs=[col(GLA_KW, 12), col(GLA_KW, 13), col(GROUP, 7), col(GROUP, 8), col(HEAD, 40),
                  const(a_w.shape), const(a_b.shape), const(norm_g.shape),
                  const(a2.shape), const(masks.shape)],
        out_specs=pl.BlockSpec((seq, GROUP), lambda b: (b, 0)),
        out_shape=jax.ShapeDtypeStruct((t, GROUP), BF16),
        scratch_shapes=[pltpu.VMEM((HEAD, GLA_KW), F32)],
        compiler_params=_cparams(("parallel",)),
        name="gla",
    )(proj, proj, proj, proj, proj, a_w, a_b, norm_g, a2, masks)


def _outproj_kernel(x_ref, cb_ref, cc_ref, ch_ref, pu_ref, cc_h_ref, ch_h_ref, pu_h_ref, cw_ref, pw_ref, ps_ref,
                    ys_ref, yg_ref, w_ref, o_ref, *, tm, tiles_per_seq):
    tile_in_seq = pl.program_id(0) % tiles_per_seq
    y_conv, y_pool = _convpool_values(cb_ref, cc_ref, ch_ref, pu_ref, cc_h_ref, ch_h_ref, pu_h_ref,
                                      cw_ref, pw_ref, ps_ref, tile_in_seq, tm)
    mixed = jnp.concatenate([y_conv, ys_ref[...], yg_ref[...], y_pool], axis=1)
    o_ref[...] = x_ref[...] + jnp.dot(mixed, w_ref[...], preferred_element_type=F32)


def _outproj(x2d, proj, y_sb, y_gla, conv_w, pool_w, pool_scale, w_out, layer, seq, tm=512):
    t, d = x2d.shape
    tm = min(tm, seq)
    hb = tm // HALO
    row = lambda width: pl.BlockSpec((tm, width), lambda i: (i, 0))
    main = lambda col: pl.BlockSpec((tm, GROUP), lambda i: (i, col))
    halo = lambda col: pl.BlockSpec((HALO, GROUP), lambda i: (jnp.maximum(i * hb - 1, 0), col))
    const = lambda shape: pl.BlockSpec(shape, lambda i: (0,) * len(shape))
    return pl.pallas_call(
        functools.partial(_outproj_kernel, tm=tm, tiles_per_seq=seq // tm),
        grid=(t // tm,),
        in_specs=[row(d), main(0), main(1), main(2), main(9), halo(1), halo(2), halo(9),
                  const(conv_w.shape), const(pool_w.shape), const(pool_scale.shape),
                  row(GROUP), row(GROUP),
                  pl.BlockSpec((None, d, d), lambda i: (layer, 0, 0))],
        out_specs=row(d),
        out_shape=jax.ShapeDtypeStruct((t, d), F32),
        compiler_params=_cparams(("parallel",)),
        name="outproj",
    )(x2d, proj, proj, proj, proj, proj, proj, proj, conv_w, pool_w, pool_scale, y_sb, y_gla, w_out)


def _ffn_kernel(x_ref, g_ref, wg_ref, wu_ref, wd_ref, o_ref, h_ref):
    @pl.when(pl.program_id(1) == 0)
    def _():
        x = x_ref[...]
        ms = jnp.mean(x * x, axis=-1, keepdims=True)
        h_ref[...] = (x * lax.rsqrt(ms + EPS) * g_ref[...]).astype(BF16)
        o_ref[...] = x

    h = h_ref[...]
    gate = jnp.dot(h, wg_ref[...], preferred_element_type=F32)
    up = jnp.dot(h, wu_ref[...], preferred_element_type=F32)
    act = (gate * jax.nn.sigmoid(gate) * up).astype(BF16)
    o_ref[...] += jnp.dot(act, wd_ref[...], preferred_element_type=F32)


def _ffn(x2d, g, wg, wu, wd, layer, tm=1024, tf=512):
    t, d = x2d.shape
    f = wg.shape[2]
    tm = min(tm, t)
    return pl.pallas_call(
        _ffn_kernel,
        grid=(t // tm, f // tf),
        in_specs=[
            pl.BlockSpec((tm, d), lambda i, j: (i, 0)),
            pl.BlockSpec((1, d), lambda i, j: (0, 0)),
            pl.BlockSpec((None, d, tf), lambda i, j: (layer, 0, j)),
            pl.BlockSpec((None, d, tf), lambda i, j: (layer, 0, j)),
            pl.BlockSpec((None, tf, d), lambda i, j: (layer, j, 0)),
        ],
        out_specs=pl.BlockSpec((tm, d), lambda i, j: (i, 0)),
        out_shape=jax.ShapeDtypeStruct((t, d), F32),
        scratch_shapes=[pltpu.VMEM((tm, d), BF16)],
        compiler_params=_cparams(("parallel", "arbitrary")),
        name="ffn",
    )(x2d, g, wg, wu, wd)


def _prep_w_in(w):
    ga0 = IN_COLS - GROUP - GLA_RANK

    def place(cols, at):
        return jnp.pad(cols, ((0, 0), (0, 0), (at, PROJ_COLS - at - cols.shape[-1])))

    return (place(w[..., :ga0], 0) + place(w[..., ga0 + GLA_RANK:], ga0)
            + place(w[..., ga0:ga0 + GLA_RANK], ga0 + GROUP)).astype(BF16)


def kernel(x, norm1_g, w_in, conv_w, sb_q_g, sb_k_g, gla_a_w, gla_a_b, gla_norm_g, pool_w, pool_scale,
           w_out, norm2_g, w_gate, w_up, w_down):
    batch, seq, d = x.shape
    depth = w_in.shape[0]
    x2d = x.reshape(batch * seq, d)
    w_in, w_out, w_gate, w_up, w_down = _prep_w_in(w_in), w_out.astype(BF16), w_gate.astype(BF16), \
        w_up.astype(BF16), w_down.astype(BF16)
    pool_w = pool_w.astype(BF16)
    a_w = jnp.pad(gla_a_w, ((0, 0), (0, HEAD - GLA_RANK), (0, 0))).astype(BF16)
    for l in range(depth):
        proj = _inproj(x2d, norm1_g[l][None, :], w_in, l)
        y_sb = _stickbreak(proj, sb_q_g[l][None, :], sb_k_g[l][None, :], batch, seq)
        y_gla = _gla(proj, a_w[l], gla_a_b[l][None, :], gla_norm_g[l][None, :], batch, seq)
        x2d = _outproj(x2d, proj, y_sb, y_gla, conv_w[l], pool_w[l], pool_scale[l][None, :], w_out, l, seq)
        x2d = _ffn(x2d, norm2_g[l][None, :], w_gate, w_up, w_down, l)
    return x2d.reshape(batch, seq, d)
```

```python
import functools

import numpy as np
import jax
import jax.numpy as jnp
from jax import lax
from jax.experimental import pallas as pl
from jax.experimental.pallas import tpu as pltpu

F32 = jnp.float32
BF16 = jnp.bfloat16

D_MODEL = 2048
GROUP = D_MODEL // 4
HEAD = 128
N_HEADS = GROUP // HEAD
GLA_DK = 64
GLA_KW = N_HEADS * GLA_DK
GLA_RANK = 16
GLA_TAU = 16.0
CHUNK = 64
N_LEVELS = 6
GLA_GROUP = 8
POOL_WINDOWS = (2, 4, 8, 16)
HALO = 16
D_FF = 5632
EPS = 1e-6
IN_COLS = 5136
PROJ_COLS = 5376
SB_BLOCK = 256

VMEM_LIMIT = 56 * 1024 * 1024


def _cparams(sem):
    return pltpu.CompilerParams(dimension_semantics=sem, vmem_limit_bytes=VMEM_LIMIT)


def _softplus(z):
    return jnp.maximum(z, 0.0) + jnp.log(1.0 + jnp.exp(-jnp.abs(z)))


def _split_bf16(a):
    hi = a.astype(BF16)
    lo = (a - hi.astype(F32)).astype(BF16)
    return hi, lo


def _inproj_kernel(x_ref, g_ref, w_ref, o_ref, h_ref):
    @pl.when(pl.program_id(1) == 0)
    def _():
        x = x_ref[...]
        ms = jnp.mean(x * x, axis=-1, keepdims=True)
        h_ref[...] = (x * lax.rsqrt(ms + EPS) * g_ref[...]).astype(BF16)

    o_ref[...] = jnp.dot(h_ref[...], w_ref[...], preferred_element_type=F32).astype(o_ref.dtype)


def _inproj(x2d, g, w, layer, tm=1024, tn=1792):
    t, d = x2d.shape
    n = w.shape[2]
    tm = min(tm, t)
    return pl.pallas_call(
        _inproj_kernel,
        grid=(t // tm, n // tn),
        in_specs=[
            pl.BlockSpec((tm, d), lambda i, j: (i, 0)),
            pl.BlockSpec((1, d), lambda i, j: (0, 0)),
            pl.BlockSpec((None, d, tn), lambda i, j: (layer, 0, j)),
        ],
        out_specs=pl.BlockSpec((tm, tn), lambda i, j: (i, j)),
        out_shape=jax.ShapeDtypeStruct((t, n), BF16),
        scratch_shapes=[pltpu.VMEM((tm, d), BF16)],
        compiler_params=_cparams(("parallel", "arbitrary")),
        name="inproj",
    )(x2d, g, w)


def _convpool_values(cb_ref, cc_ref, ch_ref, pu_ref, cc_h_ref, ch_h_ref, pu_h_ref, cw_ref, pw_ref, ps_ref,
                     tile_in_seq, ts):
    keep = (tile_in_seq > 0).astype(F32)

    def with_halo(h_ref, m_ref):
        return jnp.concatenate([h_ref[...].astype(F32) * keep, m_ref[...].astype(F32)], axis=0)

    def back(a, k):
        return pltpu.roll(a, k, axis=0)

    u = with_halo(cc_h_ref, cc_ref) * with_halo(ch_h_ref, ch_ref)
    cw = cw_ref[...]
    acc = cw[2:3, :] * u + cw[1:2, :] * back(u, 1) + cw[0:1, :] * back(u, 2)
    y_conv = (cb_ref[...].astype(F32) * acc[HALO:, :]).astype(BF16)

    p = with_halo(pu_h_ref, pu_ref)
    t_seq = tile_in_seq * ts + lax.broadcasted_iota(jnp.int32, (ts, 1), 0)
    y_pool = []
    for g, w in enumerate(POOL_WINDOWS):
        pg = p[:, g * HEAD:(g + 1) * HEAD]
        win = pg
        k = 1
        while k < w:
            win = win + back(win, k)
            k *= 2
        count = jnp.minimum(t_seq + 1, w).astype(F32)
        pooled = win[HALO:, :] / count - pg[HALO:, :]
        y = jnp.dot(pooled.astype(BF16), pw_ref[g], preferred_element_type=F32)
        y_pool.append((y * ps_ref[:, g * HEAD:(g + 1) * HEAD]).astype(BF16))
    return y_conv, jnp.concatenate(y_pool, axis=1)


def _sb_kernel(q_ref, k_ref, v_ref, qg_ref, kg_ref, m_ref, o_ref, qn_ref, kn_ref, rem_ref, acc_ref, *, seq):
    def norm(a, g):
        a = a.astype(F32)
        return a * lax.rsqrt(jnp.mean(a * a, axis=-1, keepdims=True) + EPS) * g

    scale = HEAD ** -0.5
    qn_ref[...] = (norm(q_ref[...], qg_ref[...]) * scale).astype(BF16)
    kn_ref[...] = norm(k_ref[...], kg_ref[...]).astype(BF16)
    bsz = SB_BLOCK
    row = lax.broadcasted_iota(jnp.int32, (bsz, bsz), 0)
    col = lax.broadcasted_iota(jnp.int32, (bsz, bsz), 1)
    strict = col < row

    def mask_diag(a):
        top = jnp.where(strict, a[:bsz], 0.0)
        return top if a.shape[0] == bsz else jnp.concatenate([top, a[bsz:]], axis=0)

    def column(kj):
        k0 = kj * bsz
        n = seq - k0
        kb = kn_ref[pl.ds(k0, bsz), :]
        vb = v_ref[pl.ds(k0, bsz), :]
        z = lax.dot_general(qn_ref[pl.ds(k0, n), :], kb, (((1,), (1,)), ((), ())),
                            preferred_element_type=F32)
        nlr = _softplus(z)
        log_a = z - nlr
        nlr = mask_diag(nlr)
        log_a = log_a - jnp.dot(nlr.astype(BF16), m_ref[...], preferred_element_type=F32)
        total = jnp.sum(nlr, axis=1, keepdims=True)
        lower = pl.ds(k0 + bsz, n - bsz)
        if n > bsz:
            rem = rem_ref[lower, :]
            log_a = jnp.concatenate([log_a[:bsz], log_a[bsz:] - jnp.concatenate([rem] * (bsz // HEAD), axis=1)],
                                    axis=0)
        a = mask_diag(jnp.exp(log_a))
        pv = jnp.dot(a.astype(BF16), vb, preferred_element_type=F32)
        acc_ref[pl.ds(k0, bsz), :] = pv[:bsz]
        rem_ref[pl.ds(k0, bsz), :] = jnp.broadcast_to(total[:bsz], (bsz, HEAD))
        if n > bsz:
            acc_ref[lower, :] += pv[bsz:]
            rem_ref[lower, :] = rem + total[bsz:]

    for kj in reversed(range(seq // bsz)):
        column(kj)

    o_ref[...] = acc_ref[...].astype(o_ref.dtype)


def _sb_consts():
    j = np.arange(SB_BLOCK)[:, None]
    s = np.arange(SB_BLOCK)[None, :]
    return jnp.asarray((j > s).astype(np.float32), BF16)


def _stickbreak(proj, q_g, k_g, batch, seq):
    t = proj.shape[0]
    assert seq % SB_BLOCK == 0, seq
    q0, k0, v0 = 3 * N_HEADS, 4 * N_HEADS, 5 * N_HEADS

    def col(c0):
        return pl.BlockSpec((seq, HEAD), lambda b, h: (b, c0 + h))

    const = lambda shape: pl.BlockSpec(shape, lambda b, h: (0, 0))
    return pl.pallas_call(
        functools.partial(_sb_kernel, seq=seq),
        grid=(batch, N_HEADS),
        in_specs=[col(q0), col(k0), col(v0), const((1, HEAD)), const((1, HEAD)),
                  const((SB_BLOCK, SB_BLOCK))],
        out_specs=pl.BlockSpec((seq, HEAD), lambda b, h: (b, h)),
        out_shape=jax.ShapeDtypeStruct((t, GROUP), BF16),
        scratch_shapes=[pltpu.VMEM((seq, HEAD), BF16), pltpu.VMEM((seq, HEAD), BF16),
                        pltpu.VMEM((seq, HEAD), F32), pltpu.VMEM((seq, HEAD), F32)],
        compiler_params=_cparams(("parallel", "parallel")),
        name="stickbreak",
    )(proj, proj, proj, q_g, k_g, _sb_consts())


def _gla_consts():
    c = CHUNK
    t = np.arange(c)[:, None]
    j = np.arange(c)[None, :]
    blocks = [(j <= t)]
    qf, kf, masks = [], [], []
    for l in range(N_LEVELS):
        n = 1 << l
        mid = (t // (2 * n)) * (2 * n) + n - 1
        odd = (t // n) % 2 == 1
        qf.append(odd & (j > mid) & (j <= t))
        kf.append((~odd) & (j > t) & (j <= mid))
        tt, ss = np.arange(c)[:, None], np.arange(c)[None, :]
        masks.append((tt // (2 * n) == ss // (2 * n)) & ((tt // n) % 2 == 1) & ((ss // n) % 2 == 0))
    masks.append(np.eye(c, dtype=bool))
    blocks += qf + kf + [(j > t)]
    a = np.concatenate(blocks, axis=0).astype(np.float32)
    a2 = np.concatenate([a, a], axis=1)
    m = np.stack(masks).astype(np.float32)
    m = np.tile(m, (1, 1, N_HEADS))
    return jnp.asarray(a2, BF16), jnp.asarray(m, F32)


def _gla_kernel(q_ref, k_ref, v_ref, r_ref, a_ref, aw_ref, ab_ref, ng_ref, a2_ref, m_ref,
                o_ref, st_ref, *, seq, n_grp):
    c = CHUNK
    st_ref[...] = jnp.zeros_like(st_ref)
    lane_head = lax.broadcasted_iota(jnp.int32, (1, GLA_KW), 1) // GLA_DK
    nt = (((1,), (1,)), ((), ()))
    tn = (((0,), (0,)), ((), ()))

    def stack_heads(a):
        zero = jnp.zeros_like(a)
        return jnp.concatenate([jnp.where(lane_head == h, a, zero) for h in range(N_HEADS)], axis=0)

    def block_diag_values(v):
        zero = jnp.zeros((c, HEAD), v.dtype)
        return jnp.concatenate(
            [jnp.concatenate([v[:, h * HEAD:(h + 1) * HEAD] if hh == h else zero for hh in range(N_HEADS)], axis=1)
             for h in range(N_HEADS)], axis=0)

    def e_block(e, i):
        return e[i * c:(i + 1) * c]

    def group(gi, carry):
        r0 = pl.multiple_of(gi * (n_grp * c), n_grp * c)
        rows = pl.ds(r0, n_grp * c)
        cs = range(n_grp)
        xg = jnp.dot(a_ref[rows, :], aw_ref[...], preferred_element_type=F32) + ab_ref[...]
        g = -_softplus(-xg) * (1.0 / GLA_TAU)
        g_hi, g_lo = _split_bf16(g)
        es = [jnp.dot(a2_ref[...], jnp.concatenate([e_block(g_hi, i), e_block(g_lo, i)], axis=0),
                      preferred_element_type=F32) for i in cs]
        es = [jnp.exp(e) for e in es]

        q = q_ref[rows, :].astype(F32) * (GLA_DK ** -0.5)
        k = k_ref[rows, :].astype(F32)
        v = v_ref[rows, :]
        qc = [e_block(q, i) for i in cs]
        kc = [e_block(k, i) for i in cs]
        vc = [e_block(v, i) for i in cs]

        att = [m_ref[N_LEVELS] * lax.dot_general(qc[i].astype(BF16), stack_heads(kc[i].astype(BF16)), nt,
                                                 preferred_element_type=F32) for i in cs]
        for l in range(N_LEVELS):
            ql = [(qc[i] * e_block(es[i], 1 + l)).astype(BF16) for i in cs]
            kl = [stack_heads((kc[i] * e_block(es[i], 1 + N_LEVELS + l)).astype(BF16)) for i in cs]
            p = [lax.dot_general(ql[i], kl[i], nt, preferred_element_type=F32) for i in cs]
            att = [att[i] + m_ref[l] * p[i] for i in cs]
        o = [jnp.dot(att[i].astype(BF16), block_diag_values(vc[i]), preferred_element_type=F32) for i in cs]

        qd = [(qc[i] * e_block(es[i], 0)).astype(BF16) for i in cs]
        kd = [stack_heads((kc[i] * e_block(es[i], 1 + 2 * N_LEVELS)).astype(BF16)) for i in cs]
        v_cat = [jnp.concatenate([vc[i][:, h * HEAD:(h + 1) * HEAD] for h in range(N_HEADS)], axis=0) for i in cs]
        upd = [lax.dot_general(v_cat[i], kd[i], tn, preferred_element_type=F32) for i in cs]
        st = [st_ref[...]]
        for i in cs:
            st.append(st[i] * es[i][c - 1:c, :] + upd[i])
        st_ref[...] = st[n_grp]
        o = [o[i] + lax.dot_general(qd[i], stack_heads(st[i].astype(BF16)), nt, preferred_element_type=F32)
             for i in cs]

        o = jnp.concatenate(o, axis=0)
        oh = [o[:, h * HEAD:(h + 1) * HEAD] for h in range(N_HEADS)]
        oh = [a * lax.rsqrt(jnp.mean(a * a, axis=-1, keepdims=True) + EPS) * ng_ref[...] for a in oh]
        r = r_ref[rows, :].astype(F32)
        o_ref[rows, :] = (jnp.concatenate(oh, axis=1) * (r * jax.nn.sigmoid(r))).astype(o_ref.dtype)
        return carry

    lax.fori_loop(0, seq // (n_grp * c), group, 0)


def _gla(proj, a_w, a_b, norm_g, batch, seq):
    t = proj.shape[0]
    assert seq % (GLA_GROUP * CHUNK) == 0, seq
    a2, masks = _gla_consts()

    def col(width, idx):
        return pl.BlockSpec((seq, width), lambda b: (b, idx))

    def const(shape):
        return pl.BlockSpec(shape, lambda b: (0,) * len(shape))

    return pl.pallas_call(
        functools.partial(_gla_kernel, seq=seq, n_grp=GLA_GROUP),
        grid=(batch,),
        in_specs=[col(GLA_KW, 12), col(GLA_KW, 13), col(GROUP, 7), col(GROUP, 8), col(HEAD, 40),
                  const(a_w.shape), const(a_b.shape), const(norm_g.shape),
                  const(a2.shape), const(masks.shape)],
        out_specs=pl.BlockSpec((seq, GROUP), lambda b: (b, 0)),
        out_shape=jax.ShapeDtypeStruct((t, GROUP), BF16),
        scratch_shapes=[pltpu.VMEM((HEAD, GLA_KW), F32)],
        compiler_params=_cparams(("parallel",)),
        name="gla",
    )(proj, proj, proj, proj, proj, a_w, a_b, norm_g, a2, masks)


def _outproj_kernel(x_ref, cb_ref, cc_ref, ch_ref, pu_ref, cc_h_ref, ch_h_ref, pu_h_ref, cw_ref, pw_ref, ps_ref,
                    ys_ref, yg_ref, w_ref, o_ref, *, tm, tiles_per_seq):
    tile_in_seq = pl.program_id(0) % tiles_per_seq
    y_conv, y_pool = _convpool_values(cb_ref, cc_ref, ch_ref, pu_ref, cc_h_ref, ch_h_ref, pu_h_ref,
                                      cw_ref, pw_ref, ps_ref, tile_in_seq, tm)
    mixed = jnp.concatenate([y_conv, ys_ref[...], yg_ref[...], y_pool], axis=1)
    o_ref[...] = x_ref[...] + jnp.dot(mixed, w_ref[...], preferred_element_type=F32)


def _outproj(x2d, proj, y_sb, y_gla, conv_w, pool_w, pool_scale, w_out, layer, seq, tm=512):
    t, d = x2d.shape
    tm = min(tm, seq)
    hb = tm // HALO
    row = lambda width: pl.BlockSpec((tm, width), lambda i: (i, 0))
    main = lambda col: pl.BlockSpec((tm, GROUP), lambda i: (i, col))
    halo = lambda col: pl.BlockSpec((HALO, GROUP), lambda i: (jnp.maximum(i * hb - 1, 0), col))
    const = lambda shape: pl.BlockSpec(shape, lambda i: (0,) * len(shape))
    return pl.pallas_call(
        functools.partial(_outproj_kernel, tm=tm, tiles_per_seq=seq // tm),
        grid=(t // tm,),
        in_specs=[row(d), main(0), main(1), main(2), main(9), halo(1), halo(2), halo(9),
                  const(conv_w.shape), const(pool_w.shape), const(pool_scale.shape),
                  row(GROUP), row(GROUP),
                  pl.BlockSpec((None, d, d), lambda i: (layer, 0, 0))],
        out_specs=row(d),
        out_shape=jax.ShapeDtypeStruct((t, d), F32),
        compiler_params=_cparams(("parallel",)),
        name="outproj",
    )(x2d, proj, proj, proj, proj, proj, proj, proj, conv_w, pool_w, pool_scale, y_sb, y_gla, w_out)


def _ffn_kernel(x_ref, g_ref, wg_ref, wu_ref, wd_ref, o_ref, h_ref):
    @pl.when(pl.program_id(1) == 0)
    def _():
        x = x_ref[...]
        ms = jnp.mean(x * x, axis=-1, keepdims=True)
        h_ref[...] = (x * lax.rsqrt(ms + EPS) * g_ref[...]).astype(BF16)
        o_ref[...] = x

    h = h_ref[...]
    gate = jnp.dot(h, wg_ref[...], preferred_element_type=F32)
    up = jnp.dot(h, wu_ref[...], preferred_element_type=F32)
    act = (gate * jax.nn.sigmoid(gate) * up).astype(BF16)
    o_ref[...] += jnp.dot(act, wd_ref[...], preferred_element_type=F32)


def _ffn(x2d, g, wg, wu, wd, layer, tm=1024, tf=512):
    t, d = x2d.shape
    f = wg.shape[2]
    tm = min(tm, t)
    return pl.pallas_call(
        _ffn_kernel,
        grid=(t // tm, f // tf),
        in_specs=[
            pl.BlockSpec((tm, d), lambda i, j: (i, 0)),
            pl.BlockSpec((1, d), lambda i, j: (0, 0)),
            pl.BlockSpec((None, d, tf), lambda i, j: (layer, 0, j)),
            pl.BlockSpec((None, d, tf), lambda i, j: (layer, 0, j)),
            pl.BlockSpec((None, tf, d), lambda i, j: (layer, j, 0)),
        ],
        out_specs=pl.BlockSpec((tm, d), lambda i, j: (i, 0)),
        out_shape=jax.ShapeDtypeStruct((t, d), F32),
        scratch_shapes=[pltpu.VMEM((tm, d), BF16)],
        compiler_params=_cparams(("parallel", "arbitrary")),
        name="ffn",
    )(x2d, g, wg, wu, wd)


def _prep_w_in(w):
    ga0 = IN_COLS - GROUP - GLA_RANK

    def place(cols, at):
        return jnp.pad(cols, ((0, 0), (0, 0), (at, PROJ_COLS - at - cols.shape[-1])))

    return (place(w[..., :ga0], 0) + place(w[..., ga0 + GLA_RANK:], ga0)
            + place(w[..., ga0:ga0 + GLA_RANK], ga0 + GROUP)).astype(BF16)


def kernel(x, norm1_g, w_in, conv_w, sb_q_g, sb_k_g, gla_a_w, gla_a_b, gla_norm_g, pool_w, pool_scale,
           w_out, norm2_g, w_gate, w_up, w_down):
    batch, seq, d = x.shape
    depth = w_in.shape[0]
    x2d = x.reshape(batch * seq, d)
    w_in, w_out, w_gate, w_up, w_down = _prep_w_in(w_in), w_out.astype(BF16), w_gate.astype(BF16), \
        w_up.astype(BF16), w_down.astype(BF16)
    pool_w = pool_w.astype(BF16)
    a_w = jnp.pad(gla_a_w, ((0, 0), (0, HEAD - GLA_RANK), (0, 0))).astype(BF16)
    for l in range(depth):
        proj = _inproj(x2d, norm1_g[l][None, :], w_in, l)
        y_sb = _stickbreak(proj, sb_q_g[l][None, :], sb_k_g[l][None, :], batch, seq)
        y_gla = _gla(proj, a_w[l], gla_a_b[l][None, :], gla_norm_g[l][None, :], batch, seq)
        x2d = _outproj(x2d, proj, y_sb, y_gla, conv_w[l], pool_w[l], pool_scale[l][None, :], w_out, l, seq)
        x2d = _ffn(x2d, norm2_g[l][None, :], w_gate, w_up, w_down, l)
    return x2d.reshape(batch, seq, d)
```

```python
import functools

import numpy as np
import jax
import jax.numpy as jnp
from jax import lax
from jax.experimental import pallas as pl
from jax.experimental.pallas import tpu as pltpu

F32 = jnp.float32
BF16 = jnp.bfloat16

D_MODEL = 2048
GROUP = D_MODEL // 4
HEAD = 128
N_HEADS = GROUP // HEAD
GLA_DK = 64
GLA_KW = N_HEADS * GLA_DK
GLA_RANK = 16
GLA_TAU = 16.0
CHUNK = 64
N_LEVELS = 6
GLA_GROUP = 8
POOL_WINDOWS = (2, 4, 8, 16)
HALO = 16
D_FF = 5632
EPS = 1e-6
IN_COLS = 5136
PROJ_COLS = 5376
GA_COL = 4608
SB_BLOCK = 256

VMEM_LIMIT = 56 * 1024 * 1024


def _cparams(sem):
    return pltpu.CompilerParams(dimension_semantics=sem, vmem_limit_bytes=VMEM_LIMIT)


def _softplus(z):
    return jnp.maximum(z, 0.0) + jnp.log(1.0 + jnp.exp(-jnp.abs(z)))


def _split_bf16(a):
    hi = a.astype(BF16)
    lo = (a - hi.astype(F32)).astype(BF16)
    return hi, lo


def _inproj_kernel(x_ref, g_ref, w_ref, o_ref, h_ref):
    @pl.when(pl.program_id(1) == 0)
    def _():
        x = x_ref[...]
        ms = jnp.mean(x * x, axis=-1, keepdims=True)
        h_ref[...] = (x * lax.rsqrt(ms + EPS) * g_ref[...]).astype(BF16)

    o_ref[...] = jnp.dot(h_ref[...], w_ref[...], preferred_element_type=F32).astype(o_ref.dtype)


def _inproj(x2d, g, w, layer, tm=1024, tn=1792):
    t, d = x2d.shape
    n = w.shape[2]
    tm = min(tm, t)
    return pl.pallas_call(
        _inproj_kernel,
        grid=(t // tm, n // tn),
        in_specs=[
            pl.BlockSpec((tm, d), lambda i, j: (i, 0)),
            pl.BlockSpec((1, d), lambda i, j: (0, 0)),
            pl.BlockSpec((None, d, tn), lambda i, j: (layer, 0, j)),
        ],
        out_specs=pl.BlockSpec((tm, tn), lambda i, j: (i, j)),
        out_shape=jax.ShapeDtypeStruct((t, n), BF16),
        scratch_shapes=[pltpu.VMEM((tm, d), BF16)],
        compiler_params=_cparams(("parallel", "arbitrary")),
        name="inproj",
    )(x2d, g, w)


def _convpool_values(cb_ref, cc_ref, ch_ref, pa_ref, pb_ref, cc_h_ref, ch_h_ref, pa_h_ref, pb_h_ref,
                     cw_ref, pw_ref, ps_ref, tile_in_seq, ts):
    keep = (tile_in_seq > 0).astype(F32)

    def with_halo(h_ref, m_ref):
        return jnp.concatenate([h_ref[...].astype(F32) * keep, m_ref[...].astype(F32)], axis=0)

    def back(a, k):
        return pltpu.roll(a, k, axis=0)

    u = with_halo(cc_h_ref, cc_ref) * with_halo(ch_h_ref, ch_ref)
    cw = cw_ref[...]
    acc = cw[2:3, :] * u + cw[1:2, :] * back(u, 1) + cw[0:1, :] * back(u, 2)
    y_conv = (cb_ref[...].astype(F32) * acc[HALO:, :]).astype(BF16)

    p = jnp.concatenate([with_halo(pa_h_ref, pa_ref), with_halo(pb_h_ref, pb_ref)], axis=1)
    p = pltpu.roll(p, GROUP + HEAD - GLA_RANK, axis=1)[:, :GROUP]
    t_seq = tile_in_seq * ts + lax.broadcasted_iota(jnp.int32, (ts, 1), 0)
    y_pool = []
    for g, w in enumerate(POOL_WINDOWS):
        pg = p[:, g * HEAD:(g + 1) * HEAD]
        win = pg
        k = 1
        while k < w:
            win = win + back(win, k)
            k *= 2
        count = jnp.minimum(t_seq + 1, w).astype(F32)
        pooled = win[HALO:, :] / count - pg[HALO:, :]
        y = jnp.dot(pooled.astype(BF16), pw_ref[g], preferred_element_type=F32)
        y_pool.append((y * ps_ref[:, g * HEAD:(g + 1) * HEAD]).astype(BF16))
    return y_conv, jnp.concatenate(y_pool, axis=1)


def _sb_kernel(q_ref, k_ref, v_ref, qg_ref, kg_ref, m_ref, o_ref, qn_ref, kn_ref, rem_ref, acc_ref, *, seq):
    def norm(a, g):
        a = a.astype(F32)
        return a * lax.rsqrt(jnp.mean(a * a, axis=-1, keepdims=True) + EPS) * g

    scale = HEAD ** -0.5
    qn_ref[...] = (norm(q_ref[...], qg_ref[...]) * scale).astype(BF16)
    kn_ref[...] = norm(k_ref[...], kg_ref[...]).astype(BF16)
    bsz = SB_BLOCK
    row = lax.broadcasted_iota(jnp.int32, (bsz, bsz), 0)
    col = lax.broadcasted_iota(jnp.int32, (bsz, bsz), 1)
    strict = col < row

    def mask_diag(a):
        top = jnp.where(strict, a[:bsz], 0.0)
        return top if a.shape[0] == bsz else jnp.concatenate([top, a[bsz:]], axis=0)

    def column(kj):
        k0 = kj * bsz
        n = seq - k0
        kb = kn_ref[pl.ds(k0, bsz), :]
        vb = v_ref[pl.ds(k0, bsz), :]
        z = lax.dot_general(qn_ref[pl.ds(k0, n), :], kb, (((1,), (1,)), ((), ())),
                            preferred_element_type=F32)
        nlr = _softplus(z)
        log_a = z - nlr
        nlr = mask_diag(nlr)
        log_a = log_a - jnp.dot(nlr.astype(BF16), m_ref[...], preferred_element_type=F32)
        total = jnp.sum(nlr, axis=1, keepdims=True)
        lower = pl.ds(k0 + bsz, n - bsz)
        if n > bsz:
            rem = rem_ref[lower, :]
            log_a = jnp.concatenate([log_a[:bsz], log_a[bsz:] - jnp.concatenate([rem] * (bsz // HEAD), axis=1)],
                                    axis=0)
        a = mask_diag(jnp.exp(log_a))
        pv = jnp.dot(a.astype(BF16), vb, preferred_element_type=F32)
        acc_ref[pl.ds(k0, bsz), :] = pv[:bsz]
        rem_ref[pl.ds(k0, bsz), :] = jnp.broadcast_to(total[:bsz], (bsz, HEAD))
        if n > bsz:
            acc_ref[lower, :] += pv[bsz:]
            rem_ref[lower, :] = rem + total[bsz:]

    for kj in reversed(range(seq // bsz)):
        column(kj)

    o_ref[...] = acc_ref[...].astype(o_ref.dtype)


def _sb_consts():
    j = np.arange(SB_BLOCK)[:, None]
    s = np.arange(SB_BLOCK)[None, :]
    return jnp.asarray((j > s).astype(np.float32), BF16)


def _stickbreak(proj, q_g, k_g, batch, seq):
    t = proj.shape[0]
    assert seq % SB_BLOCK == 0, seq
    q0, k0, v0 = 3 * N_HEADS, 4 * N_HEADS, 5 * N_HEADS

    def col(c0):
        return pl.BlockSpec((seq, HEAD), lambda b, h: (b, c0 + h))

    const = lambda shape: pl.BlockSpec(shape, lambda b, h: (0, 0))
    return pl.pallas_call(
        functools.partial(_sb_kernel, seq=seq),
        grid=(batch, N_HEADS),
        in_specs=[col(q0), col(k0), col(v0), const((1, HEAD)), const((1, HEAD)),
                  const((SB_BLOCK, SB_BLOCK))],
        out_specs=pl.BlockSpec((seq, HEAD), lambda b, h: (b, h)),
        out_shape=jax.ShapeDtypeStruct((t, GROUP), BF16),
        scratch_shapes=[pltpu.VMEM((seq, HEAD), BF16), pltpu.VMEM((seq, HEAD), BF16),
                        pltpu.VMEM((seq, HEAD), F32), pltpu.VMEM((seq, HEAD), F32)],
        compiler_params=_cparams(("parallel", "parallel")),
        name="stickbreak",
    )(proj, proj, proj, q_g, k_g, _sb_consts())


def _gla_consts():
    c = CHUNK
    t = np.arange(c)[:, None]
    j = np.arange(c)[None, :]
    blocks = [(j <= t)]
    qf, kf, masks = [], [], []
    for l in range(N_LEVELS):
        n = 1 << l
        mid = (t // (2 * n)) * (2 * n) + n - 1
        odd = (t // n) % 2 == 1
        qf.append(odd & (j > mid) & (j <= t))
        kf.append((~odd) & (j > t) & (j <= mid))
        tt, ss = np.arange(c)[:, None], np.arange(c)[None, :]
        masks.append((tt // (2 * n) == ss // (2 * n)) & ((tt // n) % 2 == 1) & ((ss // n) % 2 == 0))
    masks.append(np.eye(c, dtype=bool))
    blocks += qf + kf + [(j > t)]
    a = np.concatenate(blocks, axis=0).astype(np.float32)
    a2 = np.concatenate([a, a], axis=1)
    m = np.stack(masks).astype(np.float32)
    m = np.tile(m, (1, 1, N_HEADS))
    return jnp.asarray(a2, BF16), jnp.asarray(m, F32)


def _gla_kernel(q_ref, k_ref, v_ref, r_ref, a_ref, aw_ref, ab_ref, ng_ref, a2_ref, m_ref,
                o_ref, st_ref, *, seq, n_grp):
    c = CHUNK
    st_ref[...] = jnp.zeros_like(st_ref)
    lane_head = lax.broadcasted_iota(jnp.int32, (1, GLA_KW), 1) // GLA_DK
    nt = (((1,), (1,)), ((), ()))
    tn = (((0,), (0,)), ((), ()))

    def stack_heads(a):
        zero = jnp.zeros_like(a)
        return jnp.concatenate([jnp.where(lane_head == h, a, zero) for h in range(N_HEADS)], axis=0)

    def block_diag_values(v):
        zero = jnp.zeros((c, HEAD), v.dtype)
        return jnp.concatenate(
            [jnp.concatenate([v[:, h * HEAD:(h + 1) * HEAD] if hh == h else zero for hh in range(N_HEADS)], axis=1)
             for h in range(N_HEADS)], axis=0)

    def e_block(e, i):
        return e[i * c:(i + 1) * c]

    def group(gi, carry):
        r0 = pl.multiple_of(gi * (n_grp * c), n_grp * c)
        rows = pl.ds(r0, n_grp * c)
        cs = range(n_grp)
        xg = jnp.dot(a_ref[rows, :], aw_ref[...], preferred_element_type=F32) + ab_ref[...]
        g = -_softplus(-xg) * (1.0 / GLA_TAU)
        g_hi, g_lo = _split_bf16(g)
        es = [jnp.dot(a2_ref[...], jnp.concatenate([e_block(g_hi, i), e_block(g_lo, i)], axis=0),
                      preferred_element_type=F32) for i in cs]
        es = [jnp.exp(e) for e in es]

        q = q_ref[rows, :].astype(F32) * (GLA_DK ** -0.5)
        k = k_ref[rows, :].astype(F32)
        v = v_ref[rows, :]
        qc = [e_block(q, i) for i in cs]
        kc = [e_block(k, i) for i in cs]
        vc = [e_block(v, i) for i in cs]

        att = [m_ref[N_LEVELS] * lax.dot_general(qc[i].astype(BF16), stack_heads(kc[i].astype(BF16)), nt,
                                                 preferred_element_type=F32) for i in cs]
        for l in range(N_LEVELS):
            ql = [(qc[i] * e_block(es[i], 1 + l)).astype(BF16) for i in cs]
            kl = [stack_heads((kc[i] * e_block(es[i], 1 + N_LEVELS + l)).astype(BF16)) for i in cs]
            p = [lax.dot_general(ql[i], kl[i], nt, preferred_element_type=F32) for i in cs]
            att = [att[i] + m_ref[l] * p[i] for i in cs]
        o = [jnp.dot(att[i].astype(BF16), block_diag_values(vc[i]), preferred_element_type=F32) for i in cs]

        qd = [(qc[i] * e_block(es[i], 0)).astype(BF16) for i in cs]
        kd = [stack_heads((kc[i] * e_block(es[i], 1 + 2 * N_LEVELS)).astype(BF16)) for i in cs]
        v_cat = [jnp.concatenate([vc[i][:, h * HEAD:(h + 1) * HEAD] for h in range(N_HEADS)], axis=0) for i in cs]
        upd = [lax.dot_general(v_cat[i], kd[i], tn, preferred_element_type=F32) for i in cs]
        st = [st_ref[...]]
        for i in cs:
            st.append(st[i] * es[i][c - 1:c, :] + upd[i])
        st_ref[...] = st[n_grp]
        o = [o[i] + lax.dot_general(qd[i], stack_heads(st[i].astype(BF16)), nt, preferred_element_type=F32)
             for i in cs]

        o = jnp.concatenate(o, axis=0)
        oh = [o[:, h * HEAD:(h + 1) * HEAD] for h in range(N_HEADS)]
        oh = [a * lax.rsqrt(jnp.mean(a * a, axis=-1, keepdims=True) + EPS) * ng_ref[...] for a in oh]
        r = r_ref[rows, :].astype(F32)
        o_ref[rows, :] = (jnp.concatenate(oh, axis=1) * (r * jax.nn.sigmoid(r))).astype(o_ref.dtype)
        return carry

    lax.fori_loop(0, seq // (n_grp * c), group, 0)


def _gla(proj, a_w, a_b, norm_g, batch, seq):
    t = proj.shape[0]
    assert seq % (GLA_GROUP * CHUNK) == 0, seq
    a2, masks = _gla_consts()

    def col(width, idx):
        return pl.BlockSpec((seq, width), lambda b: (b, idx))

    def const(shape):
        return pl.BlockSpec(shape, lambda b: (0,) * len(shape))

    return pl.pallas_call(
        functools.partial(_gla_kernel, seq=seq, n_grp=GLA_GROUP),
        grid=(batch,),
        in_specs=[col(GLA_KW, 12), col(GLA_KW, 13), col(GROUP, 7), col(GROUP, 8), col(HEAD, GA_COL // HEAD),
                  const(a_w.shape), const(a_b.shape), const(norm_g.shape),
                  const(a2.shape), const(masks.shape)],
        out_specs=pl.BlockSpec((seq, GROUP), lambda b: (b, 0)),
        out_shape=jax.ShapeDtypeStruct((t, GROUP), BF16),
        scratch_shapes=[pltpu.VMEM((HEAD, GLA_KW), F32)],
        compiler_params=_cparams(("parallel",)),
        name="gla",
    )(proj, proj, proj, proj, proj, a_w, a_b, norm_g, a2, masks)


def _outproj_kernel(x_ref, cb_ref, cc_ref, ch_ref, pa_ref, pb_ref, cc_h_ref, ch_h_ref, pa_h_ref, pb_h_ref,
                    cw_ref, pw_ref, ps_ref, ys_ref, yg_ref, w_ref, o_ref, *, tm, tiles_per_seq):
    tile_in_seq = pl.program_id(0) % tiles_per_seq
    y_conv, y_pool = _convpool_values(cb_ref, cc_ref, ch_ref, pa_ref, pb_ref, cc_h_ref, ch_h_ref, pa_h_ref, pb_h_ref,
                                      cw_ref, pw_ref, ps_ref, tile_in_seq, tm)
    mixed = jnp.concatenate([y_conv, ys_ref[...], yg_ref[...], y_pool], axis=1)
    o_ref[...] = x_ref[...] + jnp.dot(mixed, w_ref[...], preferred_element_type=F32)


def _outproj(x2d, proj, y_sb, y_gla, conv_w, pool_w, pool_scale, w_out, layer, seq, tm=512):
    t, d = x2d.shape
    tm = min(tm, seq)
    hb = tm // HALO
    row = lambda width: pl.BlockSpec((tm, width), lambda i: (i, 0))
    main = lambda width, col: pl.BlockSpec((tm, width), lambda i: (i, col))
    halo = lambda width, col: pl.BlockSpec((HALO, width), lambda i: (jnp.maximum(i * hb - 1, 0), col))
    const = lambda shape: pl.BlockSpec(shape, lambda i: (0,) * len(shape))
    pa, pb = (GROUP, GA_COL // GROUP), (HEAD, (GA_COL + GROUP) // HEAD)
    return pl.pallas_call(
        functools.partial(_outproj_kernel, tm=tm, tiles_per_seq=seq // tm),
        grid=(t // tm,),
        in_specs=[row(d), main(GROUP, 0), main(GROUP, 1), main(GROUP, 2), main(*pa), main(*pb),
                  halo(GROUP, 1), halo(GROUP, 2), halo(*pa), halo(*pb),
                  const(conv_w.shape), const(pool_w.shape), const(pool_scale.shape),
                  row(GROUP), row(GROUP),
                  pl.BlockSpec((None, d, d), lambda i: (layer, 0, 0))],
        out_specs=row(d),
        out_shape=jax.ShapeDtypeStruct((t, d), F32),
        compiler_params=_cparams(("parallel",)),
        name="outproj",
    )(x2d, *([proj] * 9), conv_w, pool_w, pool_scale, y_sb, y_gla, w_out)


def _ffn_kernel(x_ref, g_ref, wg_ref, wu_ref, wd_ref, o_ref, h_ref):
    @pl.when(pl.program_id(1) == 0)
    def _():
        x = x_ref[...]
        ms = jnp.mean(x * x, axis=-1, keepdims=True)
        h_ref[...] = (x * lax.rsqrt(ms + EPS) * g_ref[...]).astype(BF16)
        o_ref[...] = x

    h = h_ref[...]
    gate = jnp.dot(h, wg_ref[...], preferred_element_type=F32)
    up = jnp.dot(h, wu_ref[...], preferred_element_type=F32)
    act = (gate * jax.nn.sigmoid(gate) * up).astype(BF16)
    o_ref[...] += jnp.dot(act, wd_ref[...], preferred_element_type=F32)


def _ffn(x2d, g, wg, wu, wd, layer, tm=1024, tf=512):
    t, d = x2d.shape
    f = wg.shape[2]
    tm = min(tm, t)
    return pl.pallas_call(
        _ffn_kernel,
        grid=(t // tm, f // tf),
        in_specs=[
            pl.BlockSpec((tm, d), lambda i, j: (i, 0)),
            pl.BlockSpec((1, d), lambda i, j: (0, 0)),
            pl.BlockSpec((None, d, tf), lambda i, j: (layer, 0, j)),
            pl.BlockSpec((None, d, tf), lambda i, j: (layer, 0, j)),
            pl.BlockSpec((None, tf, d), lambda i, j: (layer, j, 0)),
        ],
        out_specs=pl.BlockSpec((tm, d), lambda i, j: (i, 0)),
        out_shape=jax.ShapeDtypeStruct((t, d), F32),
        scratch_shapes=[pltpu.VMEM((tm, d), BF16)],
        compiler_params=_cparams(("parallel", "arbitrary")),
        name="ffn",
    )(x2d, g, wg, wu, wd)


def _prep_w_in(w):
    assert IN_COLS - GROUP - GLA_RANK == GA_COL
    return jnp.pad(w, ((0, 0), (0, 0), (0, PROJ_COLS - IN_COLS))).astype(BF16)


def kernel(x, norm1_g, w_in, conv_w, sb_q_g, sb_k_g, gla_a_w, gla_a_b, gla_norm_g, pool_w, pool_scale,
           w_out, norm2_g, w_gate, w_up, w_down):
    batch, seq, d = x.shape
    depth = w_in.shape[0]
    x2d = x.reshape(batch * seq, d)
    w_in, w_out, w_gate, w_up, w_down = _prep_w_in(w_in), w_out.astype(BF16), w_gate.astype(BF16), \
        w_up.astype(BF16), w_down.astype(BF16)
    pool_w = pool_w.astype(BF16)
    a_w = jnp.pad(gla_a_w, ((0, 0), (0, HEAD - GLA_RANK), (0, 0))).astype(BF16)
    for l in range(depth):
        proj = _inproj(x2d, norm1_g[l][None, :], w_in, l)
        y_sb = _stickbreak(proj, sb_q_g[l][None, :], sb_k_g[l][None, :], batch, seq)
        y_gla = _gla(proj, a_w[l], gla_a_b[l][None, :], gla_norm_g[l][None, :], batch, seq)
        x2d = _outproj(x2d, proj, y_sb, y_gla, conv_w[l], pool_w[l], pool_scale[l][None, :], w_out, l, seq)
        x2d = _ffn(x2d, norm2_g[l][None, :], w_gate, w_up, w_down, l)
    return x2d.reshape(batch, seq, d)
```

```python
import functools

import numpy as np
import jax
import jax.numpy as jnp
from jax import lax
from jax.experimental import pallas as pl
from jax.experimental.pallas import tpu as pltpu

F32 = jnp.float32
BF16 = jnp.bfloat16

D_MODEL = 2048
GROUP = D_MODEL // 4
HEAD = 128
N_HEADS = GROUP // HEAD
GLA_DK = 64
GLA_KW = N_HEADS * GLA_DK
GLA_RANK = 16
GLA_TAU = 16.0
CHUNK = 64
N_LEVELS = 6
GLA_GROUP = 8
POOL_WINDOWS = (2, 4, 8, 16)
HALO = 16
D_FF = 5632
EPS = 1e-6
IN_COLS = 5136
PROJ_COLS = 5376
GA_COL = 4608
SB_BLOCK = 256
SB_HEADS_PER_STEP = 2

VMEM_LIMIT = 56 * 1024 * 1024


def _cparams(sem):
    return pltpu.CompilerParams(dimension_semantics=sem, vmem_limit_bytes=VMEM_LIMIT)


def _softplus(z):
    return jnp.maximum(z, 0.0) + jnp.log(1.0 + jnp.exp(-jnp.abs(z)))


def _split_bf16(a):
    hi = a.astype(BF16)
    lo = (a - hi.astype(F32)).astype(BF16)
    return hi, lo


def _inproj_kernel(x_ref, g_ref, w_ref, o_ref, h_ref):
    @pl.when(pl.program_id(1) == 0)
    def _():
        x = x_ref[...]
        ms = jnp.mean(x * x, axis=-1, keepdims=True)
        h_ref[...] = (x * lax.rsqrt(ms + EPS) * g_ref[...]).astype(BF16)

    o_ref[...] = jnp.dot(h_ref[...], w_ref[...], preferred_element_type=F32).astype(o_ref.dtype)


def _inproj(x2d, g, w, layer, tm=1024, tn=1792):
    t, d = x2d.shape
    n = w.shape[2]
    tm = min(tm, t)
    return pl.pallas_call(
        _inproj_kernel,
        grid=(t // tm, n // tn),
        in_specs=[
            pl.BlockSpec((tm, d), lambda i, j: (i, 0)),
            pl.BlockSpec((1, d), lambda i, j: (0, 0)),
            pl.BlockSpec((None, d, tn), lambda i, j: (layer, 0, j)),
        ],
        out_specs=pl.BlockSpec((tm, tn), lambda i, j: (i, j)),
        out_shape=jax.ShapeDtypeStruct((t, n), BF16),
        scratch_shapes=[pltpu.VMEM((tm, d), BF16)],
        compiler_params=_cparams(("parallel", "arbitrary")),
        name="inproj",
    )(x2d, g, w)


def _convpool_values(cb_ref, cc_ref, ch_ref, pa_ref, pb_ref, cc_h_ref, ch_h_ref, pa_h_ref, pb_h_ref,
                     cw_ref, pw_ref, ps_ref, tile_in_seq, ts):
    keep = (tile_in_seq > 0).astype(F32)

    def with_halo(h_ref, m_ref):
        return jnp.concatenate([h_ref[...].astype(F32) * keep, m_ref[...].astype(F32)], axis=0)

    def back(a, k):
        return pltpu.roll(a, k, axis=0)

    u = with_halo(cc_h_ref, cc_ref) * with_halo(ch_h_ref, ch_ref)
    cw = cw_ref[...]
    acc = cw[2:3, :] * u + cw[1:2, :] * back(u, 1) + cw[0:1, :] * back(u, 2)
    y_conv = (cb_ref[...].astype(F32) * acc[HALO:, :]).astype(BF16)

    p = jnp.concatenate([with_halo(pa_h_ref, pa_ref), with_halo(pb_h_ref, pb_ref)], axis=1)
    p = pltpu.roll(p, GROUP + HEAD - GLA_RANK, axis=1)[:, :GROUP]
    t_seq = tile_in_seq * ts + lax.broadcasted_iota(jnp.int32, (ts, 1), 0)
    y_pool = []
    for g, w in enumerate(POOL_WINDOWS):
        pg = p[:, g * HEAD:(g + 1) * HEAD]
        win = pg
        k = 1
        while k < w:
            win = win + back(win, k)
            k *= 2
        count = jnp.minimum(t_seq + 1, w).astype(F32)
        pooled = win[HALO:, :] / count - pg[HALO:, :]
        y = jnp.dot(pooled.astype(BF16), pw_ref[g], preferred_element_type=F32)
        y_pool.append((y * ps_ref[:, g * HEAD:(g + 1) * HEAD]).astype(BF16))
    return y_conv, jnp.concatenate(y_pool, axis=1)


def _sb_kernel(q_ref, k_ref, v_ref, qg_ref, kg_ref, m_ref, o_ref, qn_ref, kn_ref, rem_ref, acc_ref, *, seq):
    def norm(a, g):
        a = a.astype(F32)
        return a * lax.rsqrt(jnp.mean(a * a, axis=-1, keepdims=True) + EPS) * g

    scale = HEAD ** -0.5
    heads = range(q_ref.shape[1] // HEAD)
    lanes = lambda hh: slice(hh * HEAD, (hh + 1) * HEAD)
    for hh in heads:
        qn_ref[hh] = (norm(q_ref[:, lanes(hh)], qg_ref[...]) * scale).astype(BF16)
        kn_ref[hh] = norm(k_ref[:, lanes(hh)], kg_ref[...]).astype(BF16)
    bsz = SB_BLOCK
    row = lax.broadcasted_iota(jnp.int32, (bsz, bsz), 0)
    col = lax.broadcasted_iota(jnp.int32, (bsz, bsz), 1)
    strict = col < row

    def mask_diag(a):
        top = jnp.where(strict, a[:bsz], 0.0)
        return top if a.shape[0] == bsz else jnp.concatenate([top, a[bsz:]], axis=0)

    def column(kj, hh):
        k0 = kj * bsz
        n = seq - k0
        kb = kn_ref[hh, pl.ds(k0, bsz), :]
        vb = v_ref[pl.ds(k0, bsz), lanes(hh)]
        z = lax.dot_general(qn_ref[hh, pl.ds(k0, n), :], kb, (((1,), (1,)), ((), ())),
                            preferred_element_type=F32)
        nlr = _softplus(z)
        log_a = z - nlr
        nlr = mask_diag(nlr)
        log_a = log_a - jnp.dot(nlr.astype(BF16), m_ref[...], preferred_element_type=F32)
        total = jnp.sum(nlr, axis=1, keepdims=True)
        lower = pl.ds(k0 + bsz, n - bsz)
        if n > bsz:
            rem = rem_ref[hh, lower, :]
            log_a = jnp.concatenate([log_a[:bsz], log_a[bsz:] - jnp.concatenate([rem] * (bsz // HEAD), axis=1)],
                                    axis=0)
        a = mask_diag(jnp.exp(log_a))
        pv = jnp.dot(a.astype(BF16), vb, preferred_element_type=F32)
        acc_ref[hh, pl.ds(k0, bsz), :] = pv[:bsz]
        rem_ref[hh, pl.ds(k0, bsz), :] = jnp.broadcast_to(total[:bsz], (bsz, HEAD))
        if n > bsz:
            acc_ref[hh, lower, :] += pv[bsz:]
            rem_ref[hh, lower, :] = rem + total[bsz:]

    for kj in reversed(range(seq // bsz)):
        for hh in heads:
            column(kj, hh)

    for hh in heads:
        o_ref[:, lanes(hh)] = acc_ref[hh].astype(o_ref.dtype)


def _sb_consts():
    j = np.arange(SB_BLOCK)[:, None]
    s = np.arange(SB_BLOCK)[None, :]
    return jnp.asarray((j > s).astype(np.float32), BF16)


def _stickbreak(proj, q_g, k_g, batch, seq):
    t = proj.shape[0]
    assert seq % SB_BLOCK == 0, seq
    hp = SB_HEADS_PER_STEP
    width = hp * HEAD
    q0, k0, v0 = (3 * GROUP // width, 4 * GROUP // width, 5 * GROUP // width)

    def col(c0):
        return pl.BlockSpec((seq, width), lambda b, h: (b, c0 + h))

    const = lambda shape: pl.BlockSpec(shape, lambda b, h: (0, 0))
    return pl.pallas_call(
        functools.partial(_sb_kernel, seq=seq),
        grid=(batch, N_HEADS // hp),
        in_specs=[col(q0), col(k0), col(v0), const((1, HEAD)), const((1, HEAD)),
                  const((SB_BLOCK, SB_BLOCK))],
        out_specs=pl.BlockSpec((seq, width), lambda b, h: (b, h)),
        out_shape=jax.ShapeDtypeStruct((t, GROUP), BF16),
        scratch_shapes=[pltpu.VMEM((hp, seq, HEAD), BF16), pltpu.VMEM((hp, seq, HEAD), BF16),
                        pltpu.VMEM((hp, seq, HEAD), F32), pltpu.VMEM((hp, seq, HEAD), F32)],
        compiler_params=_cparams(("parallel", "parallel")),
        name="stickbreak",
    )(proj, proj, proj, q_g, k_g, _sb_consts())


def _gla_consts():
    c = CHUNK
    t = np.arange(c)[:, None]
    j = np.arange(c)[None, :]
    blocks = [(j <= t)]
    qf, kf, masks = [], [], []
    for l in range(N_LEVELS):
        n = 1 << l
        mid = (t // (2 * n)) * (2 * n) + n - 1
        odd = (t // n) % 2 == 1
        qf.append(odd & (j > mid) & (j <= t))
        kf.append((~odd) & (j > t) & (j <= mid))
        tt, ss = np.arange(c)[:, None], np.arange(c)[None, :]
        masks.append((tt // (2 * n) == ss // (2 * n)) & ((tt // n) % 2 == 1) & ((ss // n) % 2 == 0))
    masks.append(np.eye(c, dtype=bool))
    blocks += qf + kf + [(j > t)]
    a = np.concatenate(blocks, axis=0).astype(np.float32)
    a2 = np.concatenate([a, a], axis=1)
    m = np.stack(masks).astype(np.float32)
    m = np.tile(m, (1, 1, N_HEADS))
    return jnp.asarray(a2, BF16), jnp.asarray(m, F32)


def _gla_kernel(q_ref, k_ref, v_ref, r_ref, a_ref, aw_ref, ab_ref, ng_ref, a2_ref, m_ref,
                o_ref, st_ref, *, seq, n_grp):
    c = CHUNK
    st_ref[...] = jnp.zeros_like(st_ref)
    lane_head = lax.broadcasted_iota(jnp.int32, (1, GLA_KW), 1) // GLA_DK
    nt = (((1,), (1,)), ((), ()))
    tn = (((0,), (0,)), ((), ()))

    def stack_heads(a):
        zero = jnp.zeros_like(a)
        return jnp.concatenate([jnp.where(lane_head == h, a, zero) for h in range(N_HEADS)], axis=0)

    def block_diag_values(v):
        zero = jnp.zeros((c, HEAD), v.dtype)
        return jnp.concatenate(
            [jnp.concatenate([v[:, h * HEAD:(h + 1) * HEAD] if hh == h else zero for hh in range(N_HEADS)], axis=1)
             for h in range(N_HEADS)], axis=0)

    def e_block(e, i):
        return e[i * c:(i + 1) * c]

    def group(gi, carry):
        r0 = pl.multiple_of(gi * (n_grp * c), n_grp * c)
        rows = pl.ds(r0, n_grp * c)
        cs = range(n_grp)
        xg = jnp.dot(a_ref[rows, :], aw_ref[...], preferred_element_type=F32) + ab_ref[...]
        g = -_softplus(-xg) * (1.0 / GLA_TAU)
        g_hi, g_lo = _split_bf16(g)
        es = [jnp.dot(a2_ref[...], jnp.concatenate([e_block(g_hi, i), e_block(g_lo, i)], axis=0),
                      preferred_element_type=F32) for i in cs]
        es = [jnp.exp(e) for e in es]

        q = q_ref[rows, :].astype(F32) * (GLA_DK ** -0.5)
        k = k_ref[rows, :].astype(F32)
        v = v_ref[rows, :]
        qc = [e_block(q, i) for i in cs]
        kc = [e_block(k, i) for i in cs]
        vc = [e_block(v, i) for i in cs]

        att = [m_ref[N_LEVELS] * lax.dot_general(qc[i].astype(BF16), stack_heads(kc[i].astype(BF16)), nt,
                                                 preferred_element_type=F32) for i in cs]
        for l in range(N_LEVELS):
            ql = [(qc[i] * e_block(es[i], 1 + l)).astype(BF16) for i in cs]
            kl = [stack_heads((kc[i] * e_block(es[i], 1 + N_LEVELS + l)).astype(BF16)) for i in cs]
            p = [lax.dot_general(ql[i], kl[i], nt, preferred_element_type=F32) for i in cs]
            att = [att[i] + m_ref[l] * p[i] for i in cs]
        o = [jnp.dot(att[i].astype(BF16), block_diag_values(vc[i]), preferred_element_type=F32) for i in cs]

        qd = [(qc[i] * e_block(es[i], 0)).astype(BF16) for i in cs]
        kd = [stack_heads((kc[i] * e_block(es[i], 1 + 2 * N_LEVELS)).astype(BF16)) for i in cs]
        v_cat = [jnp.concatenate([vc[i][:, h * HEAD:(h + 1) * HEAD] for h in range(N_HEADS)], axis=0) for i in cs]
        upd = [lax.dot_general(v_cat[i], kd[i], tn, preferred_element_type=F32) for i in cs]
        st = [st_ref[...]]
        for i in cs:
            st.append(st[i] * es[i][c - 1:c, :] + upd[i])
        st_ref[...] = st[n_grp]
        o = [o[i] + lax.dot_general(qd[i], stack_heads(st[i].astype(BF16)), nt, preferred_element_type=F32)
             for i in cs]

        o = jnp.concatenate(o, axis=0)
        oh = [o[:, h * HEAD:(h + 1) * HEAD] for h in range(N_HEADS)]
        oh = [a * lax.rsqrt(jnp.mean(a * a, axis=-1, keepdims=True) + EPS) * ng_ref[...] for a in oh]
        r = r_ref[rows, :].astype(F32)
        o_ref[rows, :] = (jnp.concatenate(oh, axis=1) * (r * jax.nn.sigmoid(r))).astype(o_ref.dtype)
        return carry

    lax.fori_loop(0, seq // (n_grp * c), group, 0)


def _gla(proj, a_w, a_b, norm_g, batch, seq):
    t = proj.shape[0]
    assert seq % (GLA_GROUP * CHUNK) == 0, seq
    a2, masks = _gla_consts()

    def col(width, idx):
        return pl.BlockSpec((seq, width), lambda b: (b, idx))

    def const(shape):
        return pl.BlockSpec(shape, lambda b: (0,) * len(shape))

    return pl.pallas_call(
        functools.partial(_gla_kernel, seq=seq, n_grp=GLA_GROUP),
        grid=(batch,),
        in_specs=[col(GLA_KW, 12), col(GLA_KW, 13), col(GROUP, 7), col(GROUP, 8), col(HEAD, GA_COL // HEAD),
                  const(a_w.shape), const(a_b.shape), const(norm_g.shape),
                  const(a2.shape), const(masks.shape)],
        out_specs=pl.BlockSpec((seq, GROUP), lambda b: (b, 0)),
        out_shape=jax.ShapeDtypeStruct((t, GROUP), BF16),
        scratch_shapes=[pltpu.VMEM((HEAD, GLA_KW), F32)],
        compiler_params=_cparams(("parallel",)),
        name="gla",
    )(proj, proj, proj, proj, proj, a_w, a_b, norm_g, a2, masks)


def _outproj_kernel(x_ref, cb_ref, cc_ref, ch_ref, pa_ref, pb_ref, cc_h_ref, ch_h_ref, pa_h_ref, pb_h_ref,
                    cw_ref, pw_ref, ps_ref, ys_ref, yg_ref, w_ref, o_ref, *, tm, tiles_per_seq):
    tile_in_seq = pl.program_id(0) % tiles_per_seq
    y_conv, y_pool = _convpool_values(cb_ref, cc_ref, ch_ref, pa_ref, pb_ref, cc_h_ref, ch_h_ref, pa_h_ref, pb_h_ref,
                                      cw_ref, pw_ref, ps_ref, tile_in_seq, tm)
    mixed = jnp.concatenate([y_conv, ys_ref[...], yg_ref[...], y_pool], axis=1)
    o_ref[...] = x_ref[...] + jnp.dot(mixed, w_ref[...], preferred_element_type=F32)


def _outproj(x2d, proj, y_sb, y_gla, conv_w, pool_w, pool_scale, w_out, layer, seq, tm=512):
    t, d = x2d.shape
    tm = min(tm, seq)
    hb = tm // HALO
    row = lambda width: pl.BlockSpec((tm, width), lambda i: (i, 0))
    main = lambda width, col: pl.BlockSpec((tm, width), lambda i: (i, col))
    halo = lambda width, col: pl.BlockSpec((HALO, width), lambda i: (jnp.maximum(i * hb - 1, 0), col))
    const = lambda shape: pl.BlockSpec(shape, lambda i: (0,) * len(shape))
    pa, pb = (GROUP, GA_COL // GROUP), (HEAD, (GA_COL + GROUP) // HEAD)
    return pl.pallas_call(
        functools.partial(_outproj_kernel, tm=tm, tiles_per_seq=seq // tm),
        grid=(t // tm,),
        in_specs=[row(d), main(GROUP, 0), main(GROUP, 1), main(GROUP, 2), main(*pa), main(*pb),
                  halo(GROUP, 1), halo(GROUP, 2), halo(*pa), halo(*pb),
                  const(conv_w.shape), const(pool_w.shape), const(pool_scale.shape),
                  row(GROUP), row(GROUP),
                  pl.BlockSpec((None, d, d), lambda i: (layer, 0, 0))],
        out_specs=row(d),
        out_shape=jax.ShapeDtypeStruct((t, d), F32),
        compiler_params=_cparams(("parallel",)),
        name="outproj",
    )(x2d, *([proj] * 9), conv_w, pool_w, pool_scale, y_sb, y_gla, w_out)


def _ffn_kernel(x_ref, g_ref, wg_ref, wu_ref, wd_ref, o_ref, h_ref):
    @pl.when(pl.program_id(1) == 0)
    def _():
        x = x_ref[...]
        ms = jnp.mean(x * x, axis=-1, keepdims=True)
        h_ref[...] = (x * lax.rsqrt(ms + EPS) * g_ref[...]).astype(BF16)
        o_ref[...] = x

    h = h_ref[...]
    gate = jnp.dot(h, wg_ref[...], preferred_element_type=F32)
    up = jnp.dot(h, wu_ref[...], preferred_element_type=F32)
    act = (gate * jax.nn.sigmoid(gate) * up).astype(BF16)
    o_ref[...] += jnp.dot(act, wd_ref[...], preferred_element_type=F32)


def _ffn(x2d, g, wg, wu, wd, layer, tm=1024, tf=512):
    t, d = x2d.shape
    f = wg.shape[2]
    tm = min(tm, t)
    return pl.pallas_call(
        _ffn_kernel,
        grid=(t // tm, f // tf),
        in_specs=[
            pl.BlockSpec((tm, d), lambda i, j: (i, 0)),
            pl.BlockSpec((1, d), lambda i, j: (0, 0)),
            pl.BlockSpec((None, d, tf), lambda i, j: (layer, 0, j)),
            pl.BlockSpec((None, d, tf), lambda i, j: (layer, 0, j)),
            pl.BlockSpec((None, tf, d), lambda i, j: (layer, j, 0)),
        ],
        out_specs=pl.BlockSpec((tm, d), lambda i, j: (i, 0)),
        out_shape=jax.ShapeDtypeStruct((t, d), F32),
        scratch_shapes=[pltpu.VMEM((tm, d), BF16)],
        compiler_params=_cparams(("parallel", "arbitrary")),
        name="ffn",
    )(x2d, g, wg, wu, wd)


def _prep_w_in(w):
    assert IN_COLS - GROUP - GLA_RANK == GA_COL
    return jnp.pad(w, ((0, 0), (0, 0), (0, PROJ_COLS - IN_COLS))).astype(BF16)


def kernel(x, norm1_g, w_in, conv_w, sb_q_g, sb_k_g, gla_a_w, gla_a_b, gla_norm_g, pool_w, pool_scale,
           w_out, norm2_g, w_gate, w_up, w_down):
    batch, seq, d = x.shape
    depth = w_in.shape[0]
    x2d = x.reshape(batch * seq, d)
    w_in, w_out, w_gate, w_up, w_down = _prep_w_in(w_in), w_out.astype(BF16), w_gate.astype(BF16), \
        w_up.astype(BF16), w_down.astype(BF16)
    pool_w = pool_w.astype(BF16)
    a_w = jnp.pad(gla_a_w, ((0, 0), (0, HEAD - GLA_RANK), (0, 0))).astype(BF16)
    for l in range(depth):
        proj = _inproj(x2d, norm1_g[l][None, :], w_in, l)
        y_sb = _stickbreak(proj, sb_q_g[l][None, :], sb_k_g[l][None, :], batch, seq)
        y_gla = _gla(proj, a_w[l], gla_a_b[l][None, :], gla_norm_g[l][None, :], batch, seq)
        x2d = _outproj(x2d, proj, y_sb, y_gla, conv_w[l], pool_w[l], pool_scale[l][None, :], w_out, l, seq)
        x2d = _ffn(x2d, norm2_g[l][None, :], w_gate, w_up, w_down, l)
    return x2d.reshape(batch, seq, d)
```
